```python
import math
import jax, jax.numpy as jnp
from jax import lax
import numpy as np

D_MODEL = 1024
BATCH = 4
SEQ = 8192
DEPTH = 2

GRID_W = 64
HEAD_DIM = 128
ATTN_WIDTH = D_MODEL
N_HEADS = ATTN_WIDTH // HEAD_DIM
N_KV_HEADS = 2
GROUP = N_HEADS // N_KV_HEADS
KV_WIDTH = N_KV_HEADS * HEAD_DIM
Q_BLOCK = 128
ROPE_THETA = 10000.0
ROPE_HALF = HEAD_DIM // 2
POOL_WINDOWS = (2, 4, 8, 16)
N_POOL_GROUPS = 4
POOL_WIDTH = D_MODEL // 2
POOL_GROUP_DIM = POOL_WIDTH // N_POOL_GROUPS
IN_COLS = ATTN_WIDTH + 2 * KV_WIDTH + POOL_WIDTH + 2 * D_MODEL
N_PEER_HEADS = 8
PEER_QUERY_DIM = 512
SUBKEY_DIM = PEER_QUERY_DIM // 2
N_KEYS = 128
N_EXPERTS = N_KEYS * N_KEYS
PEER_TOPK = 16
TOKEN_BLOCK = 128
N_ADA = 6
EPS = 1e-6

kernel_name = "hybrid_gqa_pool_peer_encoder"


def rms_norm(x, w):
    xf = x.astype(jnp.float32)
    y = xf * lax.rsqrt(jnp.mean(xf * xf, axis=-1, keepdims=True) + EPS)
    return (y * w.astype(jnp.float32)).astype(x.dtype)


def modulate(xn, shift, scale):
    return xn * (1.0 + scale[:, None, :]) + shift[:, None, :]


def axial_rope_angles(seq_len):
    rows = seq_len // GRID_W
    t = jnp.arange(rows * GRID_W)
    row = (t // GRID_W).astype(jnp.float32)
    col = (t % GRID_W).astype(jnp.float32)
    inv = 1.0 / (ROPE_THETA ** (jnp.arange(0, ROPE_HALF, 2, dtype=jnp.float32) / ROPE_HALF))
    return row[:, None] * inv, col[:, None] * inv


def rotate_half_dims(xh, ang):
    half = ROPE_HALF // 2
    cos = jnp.cos(ang)[None, :, None, :]
    sin = jnp.sin(ang)[None, :, None, :]
    xf = xh.astype(jnp.float32)
    x1, x2 = xf[..., :half], xf[..., half:]
    return jnp.concatenate([x1 * cos - x2 * sin, x2 * cos + x1 * sin], axis=-1).astype(xh.dtype)


def apply_axial_rope(x, ang_r, ang_c):
    return jnp.concatenate([rotate_half_dims(x[..., :ROPE_HALF], ang_r),
                            rotate_half_dims(x[..., ROPE_HALF:], ang_c)], axis=-1)


def blocked_bidirectional_gqa(q, k, v):
    B, S = q.shape[0], q.shape[1]
    nb = S // Q_BLOCK
    scale = 1.0 / math.sqrt(HEAD_DIM)
    qb = q.reshape(B, nb, Q_BLOCK, N_KV_HEADS, GROUP, HEAD_DIM).swapaxes(0, 1)

    def one_block(qblk):
        s = jnp.einsum('bqkgd,bskd->bkgqs', qblk, k).astype(jnp.float32) * scale
        p = jax.nn.softmax(s, axis=-1).astype(v.dtype)
        return jnp.einsum('bkgqs,bskd->bqkgd', p, v)

    ob = lax.map(one_block, qb)
    return ob.swapaxes(0, 1).reshape(B, S, N_HEADS * HEAD_DIM)


def multiscale_pool(u, pool_mix_w, pool_scale):
    B, S, _ = u.shape
    ug = u.reshape(B, S, N_POOL_GROUPS, POOL_GROUP_DIM).astype(jnp.float32)
    t = jnp.arange(S)
    outs = []
    for gi, w in enumerate(POOL_WINDOWS):
        xg = ug[:, :, gi, :]
        csum = jnp.concatenate([jnp.zeros((B, 1, POOL_GROUP_DIM), jnp.float32),
                                jnp.cumsum(xg, axis=1)], axis=1)
        lo = jnp.clip(t - w // 2, 0, S)
        hi = jnp.clip(t + w // 2, 0, S)
        cnt = (hi - lo).astype(jnp.float32)[None, :, None]
        outs.append((csum[:, hi] - csum[:, lo]) / cnt - xg)
    m = jnp.stack(outs, axis=2).astype(u.dtype)
    mixed = jnp.einsum('bsgc,gcd->bsgd', m, pool_mix_w).reshape(B, S, POOL_WIDTH)
    return mixed * pool_scale


def peer_ffn(xm, w_query, sub_keys, expert_u, expert_v):
    B, S, D = xm.shape
    nb = S // TOKEN_BLOCK
    xb = xm.reshape(B, nb, TOKEN_BLOCK, D).swapaxes(0, 1)

    def one_block(xblk):
        q = (xblk @ w_query).reshape(B, TOKEN_BLOCK, N_PEER_HEADS, 2, SUBKEY_DIM)
        s = jnp.einsum('bthpd,hpnd->bthpn', q, sub_keys).astype(jnp.float32)
        vals, idx = lax.top_k(s, PEER_TOPK)
        cand = (vals[..., 0, :, None] + vals[..., 1, None, :]).reshape(B, TOKEN_BLOCK, N_PEER_HEADS, PEER_TOPK * PEER_TOPK)
        cand_idx = (idx[..., 0, :, None] * N_KEYS + idx[..., 1, None, :]).reshape(B, TOKEN_BLOCK, N_PEER_HEADS, PEER_TOPK * PEER_TOPK)
        top_s, pos = lax.top_k(cand, PEER_TOPK)
        eidx = jnp.take_along_axis(cand_idx, pos, axis=-1)
        g = jax.nn.softmax(top_s, axis=-1).astype(xblk.dtype)
        u = expert_u[eidx]
        h = jnp.einsum('btd,bthkd->bthk', xblk, u)
        act = jax.nn.gelu(h, approximate=False) * g
        return jnp.einsum('bthk,bthkd->btd', act, expert_v[eidx])

    yb = lax.map(one_block, xb)
    return yb.swapaxes(0, 1).reshape(B, S, D)


def setup_inputs(seed: int = 0) -> dict:
    key = jax.random.key(seed)
    ks = jax.random.split(key, 20)
    f32 = jnp.float32
    nrm = lambda k, shape, s: (jax.random.normal(k, shape, f32) * s)
    return {
        "x": nrm(ks[0], (BATCH, SEQ, D_MODEL), 1.0),
        "c": nrm(ks[1], (BATCH, D_MODEL), 1.0),
        "ada_w": nrm(ks[2], (DEPTH, D_MODEL, N_ADA * D_MODEL), 0.5 * D_MODEL ** -0.5),
        "ada_b": nrm(ks[3], (DEPTH, N_ADA * D_MODEL), 0.01),
        "norm_mix_w": 1.0 + nrm(ks[4], (DEPTH, D_MODEL), 0.05),
        "w_in": nrm(ks[5], (DEPTH, D_MODEL, IN_COLS), D_MODEL ** -0.5),
        "q_norm_w": 1.0 + nrm(ks[6], (DEPTH, HEAD_DIM), 0.05),
        "k_norm_w": 1.0 + nrm(ks[7], (DEPTH, HEAD_DIM), 0.05),
        "w_attn_o": nrm(ks[8], (DEPTH, ATTN_WIDTH, D_MODEL), ATTN_WIDTH ** -0.5),
        "pool_mix_w": nrm(ks[9], (DEPTH, N_POOL_GROUPS, POOL_GROUP_DIM, POOL_GROUP_DIM), POOL_GROUP_DIM ** -0.5),
        "pool_scale": 1.0 + nrm(ks[10], (DEPTH, POOL_WIDTH), 0.05),
        "w_pool_o": nrm(ks[11], (DEPTH, POOL_WIDTH, D_MODEL), POOL_WIDTH ** -0.5),
        "w_out": nrm(ks[12], (DEPTH, D_MODEL, D_MODEL), D_MODEL ** -0.5),
        "norm_ffn_w": 1.0 + nrm(ks[13], (DEPTH, D_MODEL), 0.05),
        "w_query": nrm(ks[14], (DEPTH, D_MODEL, N_PEER_HEADS * PEER_QUERY_DIM), D_MODEL ** -0.5),
        "sub_keys": nrm(ks[15], (DEPTH, N_PEER_HEADS, 2, N_KEYS, SUBKEY_DIM), SUBKEY_DIM ** -0.5),
        "expert_u": nrm(ks[16], (DEPTH, N_EXPERTS, D_MODEL), D_MODEL ** -0.5),
        "expert_v": nrm(ks[17], (DEPTH, N_EXPERTS, D_MODEL), N_PEER_HEADS ** -0.5),
        "final_norm_w": 1.0 + nrm(ks[18], (D_MODEL,), 0.05),
    }


def reference(x, c, ada_w, ada_b, norm_mix_w, w_in, q_norm_w, k_norm_w, w_attn_o, pool_mix_w,
              pool_scale, w_pool_o, w_out, norm_ffn_w, w_query, sub_keys, expert_u, expert_v,
              final_norm_w):
    B, S, _ = x.shape
    ang_r, ang_c = axial_rope_angles(S)
    c_act = jax.nn.silu(c)
    splits = [ATTN_WIDTH, ATTN_WIDTH + KV_WIDTH, ATTN_WIDTH + 2 * KV_WIDTH,
              ATTN_WIDTH + 2 * KV_WIDTH + POOL_WIDTH, ATTN_WIDTH + 2 * KV_WIDTH + POOL_WIDTH + D_MODEL]
    for l in range(DEPTH):
        mod = c_act @ ada_w[l] + ada_b[l]
        sh1, sc1, g1, sh2, sc2, g2 = jnp.split(mod, N_ADA, axis=-1)

        xm = modulate(rms_norm(x, norm_mix_w[l]), sh1, sc1)
        proj = xm @ w_in[l]
        q, k, v, u_pool, ga, gp = jnp.split(proj, splits, axis=-1)

        q = rms_norm(q.reshape(B, S, N_HEADS, HEAD_DIM), q_norm_w[l])
        k = rms_norm(k.reshape(B, S, N_KV_HEADS, HEAD_DIM), k_norm_w[l])
        v = v.reshape(B, S, N_KV_HEADS, HEAD_DIM)
        q = apply_axial_rope(q, ang_r, ang_c).reshape(B, S, N_KV_HEADS, GROUP, HEAD_DIM)
        k = apply_axial_rope(k, ang_r, ang_c)
        attn = blocked_bidirectional_gqa(q, k, v) @ w_attn_o[l]

        pool = multiscale_pool(u_pool, pool_mix_w[l], pool_scale[l]) @ w_pool_o[l]

        merged = jax.nn.sigmoid(ga) * attn + jax.nn.sigmoid(gp) * pool
        x = x + g1[:, None, :] * (merged @ w_out[l])

        xf = modulate(rms_norm(x, norm_ffn_w[l]), sh2, sc2)
        x = x + g2[:, None, :] * peer_ffn(xf, w_query[l], sub_keys[l], expert_u[l], expert_v[l])

    return rms_norm(x, final_norm_w)
```

```python
import functools
import math

import jax
import jax.numpy as jnp
import numpy as np
from jax import lax
from jax.experimental import pallas as pl
from jax.experimental.pallas import tpu as pltpu

F32 = jnp.float32
BF16 = jnp.bfloat16

EPS = 1e-6
GRID_W = 64
HEAD_DIM = 128
N_HEADS = 8
N_KV_HEADS = 2
GROUP = N_HEADS // N_KV_HEADS
ROPE_THETA = 10000.0
ROPE_HALF = HEAD_DIM // 2
POOL_WINDOWS = (2, 4, 8, 16)
POOL_GROUP_DIM = 128
POOL_HALO = 8
N_PEER_HEADS = 8
SUBKEY_DIM = 256
N_KEYS = 128
PEER_TOPK = 16
N_ADA = 6
NOT_SELECTED_RANK = 99.0

VMEM_LIMIT_BYTES = 56 * 1024 * 1024


def _cparams(*sem):
    return pltpu.CompilerParams(dimension_semantics=sem, vmem_limit_bytes=VMEM_LIMIT_BYTES)


def _nt_dot(a, b):
    return lax.dot_general(a, b, (((1,), (1,)), ((), ())), preferred_element_type=F32)


def _ada_kernel(c_ref, w_ref, b_ref, o_ref):
    c = c_ref[...]
    c_act = c * jax.nn.sigmoid(c)
    o_ref[0] = jnp.dot(c_act, w_ref[0], preferred_element_type=F32,
                       precision=lax.Precision.HIGHEST) + b_ref[0]


def _ada_mod(c_pad, ada_w, ada_b):
    depth, d, n = ada_w.shape
    rows = c_pad.shape[0]
    tn = 1536
    return pl.pallas_call(
        _ada_kernel,
        grid=(depth, n // tn),
        in_specs=[
            pl.BlockSpec((rows, d), lambda l, j: (0, 0)),
            pl.BlockSpec((1, d, tn), lambda l, j: (l, 0, j)),
            pl.BlockSpec((1, 1, tn), lambda l, j: (l, 0, j)),
        ],
        out_specs=pl.BlockSpec((1, rows, tn), lambda l, j: (l, 0, j)),
        out_shape=jax.ShapeDtypeStruct((depth, rows, n), F32),
        compiler_params=_cparams("parallel", "parallel"),
        name="ada_mod",
    )(c_pad, ada_w, ada_b.reshape(depth, 1, n))


def _rms_rope(h, nw, cos, sin, low_half):
    hn = h * lax.rsqrt(jnp.mean(h * h, axis=-1, keepdims=True) + EPS) * nw
    swapped = jnp.where(low_half, pltpu.roll(hn, 96, 1), pltpu.roll(hn, 32, 1))
    return hn * cos + swapped * sin


def _in_proj_kernel(x_ref, mod_ref, nw_ref, w_ref, qnw_ref, knw_ref, cos_ref, sin_ref,
                    q_ref, k_ref, v_ref, u_ref, ga_ref, gp_ref, *, d_model):
    x = x_ref[0]
    xn = x * lax.rsqrt(jnp.mean(x * x, axis=-1, keepdims=True) + EPS) * nw_ref[...]
    shift = mod_ref[0, 0:1, :]
    scale = mod_ref[0, 1:2, :]
    xm = (xn * (1.0 + scale) + shift).astype(BF16)
    proj = jnp.dot(xm, w_ref[...], preferred_element_type=F32)

    cos = cos_ref[...]
    sin = sin_ref[...]
    lane = lax.broadcasted_iota(jnp.int32, cos.shape, 1)
    low_half = (lane % ROPE_HALF) < (ROPE_HALF // 2)
    sm_scale = 1.0 / math.sqrt(HEAD_DIM)
    attn_w = N_HEADS * HEAD_DIM
    kv_w = N_KV_HEADS * HEAD_DIM
    for h in range(N_HEADS):
        qh = _rms_rope(proj[:, h * HEAD_DIM:(h + 1) * HEAD_DIM], qnw_ref[...], cos, sin, low_half)
        q_ref[0, :, h * HEAD_DIM:(h + 1) * HEAD_DIM] = (qh * sm_scale).astype(BF16)
    for h in range(N_KV_HEADS):
        c0 = attn_w + h * HEAD_DIM
        kh = _rms_rope(proj[:, c0:c0 + HEAD_DIM], knw_ref[...], cos, sin, low_half)
        k_ref[0, :, h * HEAD_DIM:(h + 1) * HEAD_DIM] = kh.astype(BF16)
    c0 = attn_w + kv_w
    v_ref[0] = proj[:, c0:c0 + kv_w].astype(BF16)
    c0 += kv_w
    pool_w = u_ref.shape[-1]
    u_ref[0] = proj[:, c0:c0 + pool_w]
    c0 += pool_w
    ga_ref[0] = jax.nn.sigmoid(proj[:, c0:c0 + d_model]).astype(BF16)
    c0 += d_model
    gp_ref[0] = jax.nn.sigmoid(proj[:, c0:c0 + d_model]).astype(BF16)


def _in_proj(x, mod, nw, w_bf, qnw, knw, cos_t, sin_t, *, tile):
    b, s, d = x.shape
    in_cols = w_bf.shape[1]
    attn_w = N_HEADS * HEAD_DIM
    kv_w = N_KV_HEADS * HEAD_DIM
    pool_w = in_cols - attn_w - 2 * kv_w - 2 * d
    tok = lambda width: pl.BlockSpec((1, tile, width), lambda bi, i: (bi, i, 0))
    full = lambda a: pl.BlockSpec(a.shape, lambda bi, i: (0,) * a.ndim)
    return pl.pallas_call(
        functools.partial(_in_proj_kernel, d_model=d),
        grid=(b, s // tile),
        in_specs=[
            tok(d),
            pl.BlockSpec((1, N_ADA, d), lambda bi, i: (bi, 0, 0)),
            full(nw), full(w_bf), full(qnw), full(knw),
            pl.BlockSpec((tile, HEAD_DIM), lambda bi, i: (i, 0)),
            pl.BlockSpec((tile, HEAD_DIM), lambda bi, i: (i, 0)),
        ],
        out_specs=[tok(attn_w), tok(kv_w), tok(kv_w), tok(pool_w), tok(d), tok(d)],
        out_shape=[
            jax.ShapeDtypeStruct((b, s, attn_w), BF16),
            jax.ShapeDtypeStruct((b, s, kv_w), BF16),
            jax.ShapeDtypeStruct((b, s, kv_w), BF16),
            jax.ShapeDtypeStruct((b, s, pool_w), F32),
            jax.ShapeDtypeStruct((b, s, d), BF16),
            jax.ShapeDtypeStruct((b, s, d), BF16),
        ],
        compiler_params=_cparams("parallel", "parallel"),
        name="in_proj",
    )(x, mod, nw, w_bf, qnw, knw, cos_t, sin_t)


def _flash_kernel(q_ref, k_ref, v_ref, o_ref, m_scr, l_scr, acc_scr, *, tq, tk, seq):
    q4 = jnp.concatenate(
        [q_ref[0, :, g * HEAD_DIM:(g + 1) * HEAD_DIM] for g in range(GROUP)], axis=0)
    m_scr[...] = jnp.full(m_scr.shape, -jnp.inf, F32)
    l_scr[...] = jnp.zeros(l_scr.shape, F32)
    acc_scr[...] = jnp.zeros(acc_scr.shape, F32)

    def body(j, carry):
        off = pl.multiple_of(j * tk, tk)
        kc = k_ref[0, pl.ds(off, tk), :]
        vc = v_ref[0, pl.ds(off, tk), :]
        s = _nt_dot(q4, kc)
        m_prev = m_scr[...]
        m_new = jnp.maximum(m_prev, jnp.max(s, axis=-1, keepdims=True))
        alpha = jnp.exp(m_prev - m_new)
        p = jnp.exp(s - m_new)
        l_scr[...] = alpha * l_scr[...] + jnp.sum(p, axis=-1, keepdims=True)
        acc_scr[...] = alpha * acc_scr[...] + jnp.dot(p.astype(BF16), vc,
                                                     preferred_element_type=F32)
        m_scr[...] = m_new
        return carry

    lax.fori_loop(0, seq // tk, body, 0)
    out = acc_scr[...] / l_scr[...]
    for g in range(GROUP):
        o_ref[0, :, g * HEAD_DIM:(g + 1) * HEAD_DIM] = out[g * tq:(g + 1) * tq].astype(BF16)


def _flash_attention(q, k, v, *, tq, tk):
    b, s, _ = q.shape
    gw = GROUP * HEAD_DIM
    return pl.pallas_call(
        functools.partial(_flash_kernel, tq=tq, tk=tk, seq=s),
        grid=(b, N_KV_HEADS, s // tq),
        in_specs=[
            pl.BlockSpec((1, tq, gw), lambda bi, kv, i: (bi, i, kv)),
            pl.BlockSpec((1, s, HEAD_DIM), lambda bi, kv, i: (bi, 0, kv)),
            pl.BlockSpec((1, s, HEAD_DIM), lambda bi, kv, i: (bi, 0, kv)),
        ],
        out_specs=pl.BlockSpec((1, tq, gw), lambda bi, kv, i: (bi, i, kv)),
        out_shape=jax.ShapeDtypeStruct(q.shape, BF16),
        scratch_shapes=[
            pltpu.VMEM((GROUP * tq, 1), F32),
            pltpu.VMEM((GROUP * tq, 1), F32),
            pltpu.VMEM((GROUP * tq, HEAD_DIM), F32),
        ],
        compiler_params=_cparams("parallel", "parallel", "parallel"),
        name="flash_gqa",
    )(q, k, v)


def _mix_kernel(attn_ref, u_ref, up_ref, un_ref, ga_ref, gp_ref, x_ref, mod_ref,
                wao_ref, pmw_ref, ps_ref, wpo_ref, wout_ref, o_ref, ext_scr, *, tile, seq):
    i = pl.program_id(1)
    nblk = pl.num_programs(1)
    pool_w = u_ref.shape[-1]
    prev_ok = (i > 0).astype(F32)
    next_ok = (i < nblk - 1).astype(F32)
    ext_scr[0:POOL_HALO, :] = up_ref[0] * prev_ok
    ext_scr[POOL_HALO:POOL_HALO + tile, :] = u_ref[0]
    ext_scr[POOL_HALO + tile:POOL_HALO + tile + POOL_HALO, :] = un_ref[0] * next_ok

    t = i * tile + lax.broadcasted_iota(jnp.int32, (tile, POOL_GROUP_DIM), 0)
    mixed = []
    for gi, w in enumerate(POOL_WINDOWS):
        cols = slice(gi * POOL_GROUP_DIM, (gi + 1) * POOL_GROUP_DIM)
        half = w // 2
        win = ext_scr[POOL_HALO - half:POOL_HALO - half + tile, cols]
        for kk in range(1, w):
            a = POOL_HALO - half + kk
            win = win + ext_scr[a:a + tile, cols]
        cnt = (jnp.minimum(t + half, seq) - jnp.maximum(t - half, 0)).astype(F32)
        m = win / cnt - u_ref[0, :, cols]
        mg = jnp.dot(m.astype(BF16), pmw_ref[gi], preferred_element_type=F32)
        mixed.append((mg * ps_ref[:, cols]).astype(BF16))
    pool_in = jnp.concatenate(mixed, axis=1)
    pool_o = jnp.dot(pool_in, wpo_ref[...], preferred_element_type=F32)
    attn_o = jnp.dot(attn_ref[0], wao_ref[...], preferred_element_type=F32)
    merged = ga_ref[0].astype(F32) * attn_o + gp_ref[0].astype(F32) * pool_o
    y = jnp.dot(merged.astype(BF16), wout_ref[...], preferred_element_type=F32)
    g1 = mod_ref[0, 2:3, :]
    o_ref[0] = x_ref[0] + g1 * y
    del pool_w


def _mix_out(attn, u, sga, sgp, x, mod, wao, pmw, ps, wpo, wout, *, tile):
    b, s, d = x.shape
    pool_w = u.shape[-1]
    hb = tile // POOL_HALO
    n_halo = s // POOL_HALO
    tok = lambda width: pl.BlockSpec((1, tile, width), lambda bi, i: (bi, i, 0))
    full = lambda a: pl.BlockSpec(a.shape, lambda bi, i: (0,) * a.ndim)
    return pl.pallas_call(
        functools.partial(_mix_kernel, tile=tile, seq=s),
        grid=(b, s // tile),
        in_specs=[
            tok(d), tok(pool_w),
            pl.BlockSpec((1, POOL_HALO, pool_w),
                         lambda bi, i: (bi, jnp.maximum(i * hb - 1, 0), 0)),
            pl.BlockSpec((1, POOL_HALO, pool_w),
                         lambda bi, i: (bi, jnp.minimum((i + 1) * hb, n_halo - 1), 0)),
            tok(d), tok(d), tok(d),
            pl.BlockSpec((1, N_ADA, d), lambda bi, i: (bi, 0, 0)),
            full(wao), full(pmw), full(ps), full(wpo), full(wout),
        ],
        out_specs=tok(d),
        out_shape=jax.ShapeDtypeStruct(x.shape, F32),
        scratch_shapes=[pltpu.VMEM((tile + 2 * POOL_HALO, pool_w), F32)],
        compiler_params=_cparams("parallel", "parallel"),
        name="mix_out",
    )(attn, u, u, u, sga, sgp, x, mod, wao, pmw, ps, wpo, wout)


_CAND_B_COUNTS = (16, 8, 8, 8, 8, 8, 8, 8)


def _top16_rows(s, row_iota, record_rank):
    vals, idxs = [], []
    rank = jnp.full(s.shape, NOT_SELECTED_RANK, F32)
    work = s
    big = float(s.shape[0])
    for r in range(PEER_TOPK):
        m = jnp.max(work, axis=0, keepdims=True)
        pos = jnp.min(jnp.where(work == m, row_iota, big), axis=0, keepdims=True)
        sel = row_iota == pos
        vals.append(m)
        idxs.append(pos)
        if record_rank:
            rank = jnp.where(sel, float(r), rank)
        work = jnp.where(sel, -jnp.inf, work)
    return jnp.concatenate(vals, axis=0), jnp.concatenate(idxs, axis=0), rank


def _route_kernel(x_ref, mod_ref, nw_ref, wq_ref, sk_ref,
                  xf_ref, a0_ref, n0_ref, a1_ref, r1_ref, *, tile):
    x = x_ref[0]
    xn = x * lax.rsqrt(jnp.mean(x * x, axis=-1, keepdims=True) + EPS) * nw_ref[...]
    shift = mod_ref[0, 3:4, :]
    scale = mod_ref[0, 4:5, :]
    xf = (xn * (1.0 + scale) + shift).astype(BF16)
    xf_ref[0] = xf
    q_t = _nt_dot(wq_ref[...], xf)

    key_iota = lax.broadcasted_iota(jnp.int32, (N_KEYS, tile), 0).astype(F32)
    n_cand = sum(_CAND_B_COUNTS) + (PEER_TOPK - len(_CAND_B_COUNTS))
    cand_iota = lax.broadcasted_iota(jnp.int32, (n_cand, tile), 0).astype(F32)

    for h in range(N_PEER_HEADS):
        scores = []
        for p in range(2):
            r0 = (h * 2 + p) * SUBKEY_DIM
            qs = q_t[r0:r0 + SUBKEY_DIM, :].astype(BF16)
            scores.append(jnp.dot(sk_ref[h, p], qs, preferred_element_type=F32))
        v0, i0, rank0 = _top16_rows(scores[0], key_iota, True)
        v1, _, rank1 = _top16_rows(scores[1], key_iota, True)
        del i0

        blocks = []
        for a, nb in enumerate(_CAND_B_COUNTS):
            blocks.append(v0[a:a + 1, :] + v1[0:nb, :])
        blocks.append(v0[len(_CAND_B_COUNTS):, :] + v1[0:1, :])
        cand = jnp.concatenate(blocks, axis=0)
        top = cand[0:1, :]
        work = cand
        selected = jnp.zeros(cand.shape, F32)
        for _ in range(PEER_TOPK):
            m = jnp.max(work, axis=0, keepdims=True)
            pos = jnp.min(jnp.where(work == m, cand_iota, float(n_cand)), axis=0, keepdims=True)
            sel = cand_iota == pos
            selected = jnp.where(sel, 1.0, selected)
            work = jnp.where(sel, -jnp.inf, work)
        z = jnp.sum(selected * jnp.exp(cand - top), axis=0, keepdims=True)
        log_z = jnp.log(z)

        n0 = jnp.zeros((N_KEYS, tile), F32)
        row = 0
        for a, nb in enumerate(_CAND_B_COUNTS):
            n_a = jnp.sum(selected[row:row + nb, :], axis=0, keepdims=True)
            n0 = jnp.where(rank0 == float(a), n_a, n0)
            row += nb
        for a in range(len(_CAND_B_COUNTS), PEER_TOPK):
            n0 = jnp.where(rank0 == float(a), selected[row:row + 1, :], n0)
            row += 1

        a0_ref[h] = scores[0] - v0[0:1, :] - log_z
        n0_ref[h] = n0
        a1_ref[h] = scores[1] - v1[0:1, :]
        r1_ref[h] = rank1


def _peer_route(x1, mod, nw, wq_t, sk_bf, *, tile):
    b, s, d = x1.shape
    n_tok = b * s
    nblk = s // tile
    dense = lambda: pl.BlockSpec((N_PEER_HEADS, N_KEYS, tile), lambda bi, i: (0, 0, bi * nblk + i))
    full = lambda a: pl.BlockSpec(a.shape, lambda bi, i: (0,) * a.ndim)
    dense_shape = jax.ShapeDtypeStruct((N_PEER_HEADS, N_KEYS, n_tok), F32)
    return pl.pallas_call(
        functools.partial(_route_kernel, tile=tile),
        grid=(b, nblk),
        in_specs=[
            pl.BlockSpec((1, tile, d), lambda bi, i: (bi, i, 0)),
            pl.BlockSpec((1, N_ADA, d), lambda bi, i: (bi, 0, 0)),
            full(nw), full(wq_t), full(sk_bf),
        ],
        out_specs=[pl.BlockSpec((1, tile, d), lambda bi, i: (bi, i, 0)),
                   dense(), dense(), dense(), dense()],
        out_shape=[jax.ShapeDtypeStruct((b, s, d), BF16),
                   dense_shape, dense_shape, dense_shape, dense_shape],
        compiler_params=_cparams("parallel", "parallel"),
        name="peer_route",
    )(x1, mod, nw, wq_t, sk_bf)


def _expert_kernel(xf_ref, a0_ref, n0_ref, a1_ref, r1_ref, u_ref, vt_ref, x_ref, mod_ref, fnw_ref,
                   o_ref, h_scr, act_scr, y_scr, *, tile, keys_per_chunk, final_norm):
    ec = pl.program_id(2)
    n_chunks = pl.num_programs(2)

    @pl.when(ec == 0)
    def _():
        y_scr[...] = jnp.zeros(y_scr.shape, F32)

    h_scr[...] = _nt_dot(u_ref[...], xf_ref[0])

    for ii in range(keys_per_chunk):
        key = ec * keys_per_chunk + ii
        gate = jnp.zeros((N_KEYS, tile), F32)
        for h in range(N_PEER_HEADS):
            a0 = a0_ref[h, pl.ds(key, 1), :]
            n0 = n0_ref[h, pl.ds(key, 1), :]
            logit = a0 + a1_ref[h]
            gate = gate + jnp.where(r1_ref[h] < n0, jnp.exp(logit), 0.0)
        hid = h_scr[ii * N_KEYS:(ii + 1) * N_KEYS, :]
        act = 0.5 * hid * (1.0 + lax.erf(hid * (1.0 / math.sqrt(2.0))))
        act_scr[ii * N_KEYS:(ii + 1) * N_KEYS, :] = (act * gate).astype(BF16)

    y_scr[...] += jnp.dot(vt_ref[...], act_scr[...], preferred_element_type=F32)

    @pl.when(ec == n_chunks - 1)
    def _():
        g2 = mod_ref[0, 5:6, :]
        out = x_ref[0] + g2 * y_scr[...].T
        if final_norm:
            out = out * lax.rsqrt(jnp.mean(out * out, axis=-1, keepdims=True) + EPS) * fnw_ref[...]
        o_ref[0] = out


def _peer_experts(xf, a0, n0, a1, r1, u_bf, vt_bf, x1, mod, fnw, *, tile, keys_per_chunk,
                  final_norm):
    b, s, d = x1.shape
    nblk = s // tile
    n_exp = u_bf.shape[0]
    chunk = keys_per_chunk * N_KEYS
    dense = lambda: pl.BlockSpec((N_PEER_HEADS, N_KEYS, tile),
                                 lambda bi, i, e: (0, 0, bi * nblk + i))
    tok = lambda: pl.BlockSpec((1, tile, d), lambda bi, i, e: (bi, i, 0))
    return pl.pallas_call(
        functools.partial(_expert_kernel, tile=tile, keys_per_chunk=keys_per_chunk,
                          final_norm=final_norm),
        grid=(b, nblk, n_exp // chunk),
        in_specs=[
            tok(), dense(), dense(), dense(), dense(),
            pl.BlockSpec((chunk, d), lambda bi, i, e: (e, 0)),
            pl.BlockSpec((d, chunk), lambda bi, i, e: (0, e)),
            tok(),
            pl.BlockSpec((1, N_ADA, d), lambda bi, i, e: (bi, 0, 0)),
            pl.BlockSpec(fnw.shape, lambda bi, i, e: (0, 0)),
        ],
        out_specs=tok(),
        out_shape=jax.ShapeDtypeStruct(x1.shape, F32),
        scratch_shapes=[
            pltpu.VMEM((chunk, tile), F32),
            pltpu.VMEM((chunk, tile), BF16),
            pltpu.VMEM((d, tile), F32),
        ],
        compiler_params=_cparams("parallel", "parallel", "arbitrary"),
        name="peer_experts",
    )(xf, a0, n0, a1, r1, u_bf, vt_bf, x1, mod, fnw)


def _rope_tables(seq):
    t = jnp.arange(seq)
    row = (t // GRID_W).astype(F32)
    col = (t % GRID_W).astype(F32)
    inv = 1.0 / (ROPE_THETA ** (jnp.arange(0, ROPE_HALF, 2, dtype=F32) / ROPE_HALF))
    ang_r = row[:, None] * inv
    ang_c = col[:, None] * inv
    cos_t = jnp.concatenate([jnp.cos(ang_r), jnp.cos(ang_r), jnp.cos(ang_c), jnp.cos(ang_c)], axis=1)
    sin_t = jnp.concatenate([-jnp.sin(ang_r), jnp.sin(ang_r), -jnp.sin(ang_c), jnp.sin(ang_c)], axis=1)
    return cos_t, sin_t


def _pick_tile(seq, want):
    tile = min(want, seq)
    assert seq % tile == 0
    return tile


def kernel(x, c, ada_w, ada_b, norm_mix_w, w_in, q_norm_w, k_norm_w, w_attn_o, pool_mix_w, pool_scale, w_pool_o, w_out, norm_ffn_w, w_query, sub_keys, expert_u, expert_v, final_norm_w):
    b, s, d = x.shape
    depth = ada_w.shape[0]
    cos_t, sin_t = _rope_tables(s)

    rows = 8
    c_pad = jnp.zeros((rows, d), F32).at[:b].set(c)
    mod_all = _ada_mod(c_pad, ada_w, ada_b)[:, :b].reshape(depth, b, N_ADA, d)

    t_proj = _pick_tile(s, 512)
    t_q = _pick_tile(s, 256)
    t_k = _pick_tile(s, 512)
    t_mix = _pick_tile(s, 512)
    t_route = _pick_tile(s, 256)
    t_exp = _pick_tile(s, 512)

    for l in range(depth):
        mod = mod_all[l]
        q, k, v, u, sga, sgp = _in_proj(
            x, mod, norm_mix_w[l][None], w_in[l].astype(BF16), q_norm_w[l][None],
            k_norm_w[l][None], cos_t, sin_t, tile=t_proj)
        attn = _flash_attention(q, k, v, tq=t_q, tk=t_k)
        x1 = _mix_out(attn, u, sga, sgp, x, mod, w_attn_o[l].astype(BF16),
                      pool_mix_w[l].astype(BF16), pool_scale[l][None],
                      w_pool_o[l].astype(BF16), w_out[l].astype(BF16), tile=t_mix)
        xf, a0, n0, a1, r1 = _peer_route(
            x1, mod, norm_ffn_w[l][None], w_query[l].T.astype(BF16),
            sub_keys[l].astype(BF16), tile=t_route)
        x = _peer_experts(
            xf, a0, n0, a1, r1, expert_u[l].astype(BF16), expert_v[l].T.astype(BF16), x1, mod,
            final_norm_w[None], tile=t_exp, keys_per_chunk=8, final_norm=(l == depth - 1))
    return x
```

```python
import functools
import math

import jax
import jax.numpy as jnp
import numpy as np
from jax import lax
from jax.experimental import pallas as pl
from jax.experimental.pallas import tpu as pltpu

F32 = jnp.float32
BF16 = jnp.bfloat16

EPS = 1e-6
GRID_W = 64
HEAD_DIM = 128
N_HEADS = 8
N_KV_HEADS = 2
GROUP = N_HEADS // N_KV_HEADS
ROPE_THETA = 10000.0
ROPE_HALF = HEAD_DIM // 2
POOL_WINDOWS = (2, 4, 8, 16)
POOL_GROUP_DIM = 128
POOL_HALO = 8
N_PEER_HEADS = 8
SUBKEY_DIM = 256
N_KEYS = 128
PEER_TOPK = 16
N_ADA = 6
NOT_SELECTED_RANK = 99.0

VMEM_LIMIT_BYTES = 56 * 1024 * 1024


def _cparams(*sem):
    return pltpu.CompilerParams(dimension_semantics=sem, vmem_limit_bytes=VMEM_LIMIT_BYTES)


def _nt_dot(a, b):
    return lax.dot_general(a, b, (((1,), (1,)), ((), ())), preferred_element_type=F32)


def _ada_kernel(c_ref, w_ref, b_ref, o_ref):
    c = c_ref[...]
    c_act = c * jax.nn.sigmoid(c)
    o_ref[0] = jnp.dot(c_act, w_ref[0], preferred_element_type=F32,
                       precision=lax.Precision.HIGHEST) + b_ref[0]


def _ada_mod(c_pad, ada_w, ada_b):
    depth, d, n = ada_w.shape
    rows = c_pad.shape[0]
    tn = 1536
    return pl.pallas_call(
        _ada_kernel,
        grid=(depth, n // tn),
        in_specs=[
            pl.BlockSpec((rows, d), lambda l, j: (0, 0)),
            pl.BlockSpec((1, d, tn), lambda l, j: (l, 0, j)),
            pl.BlockSpec((1, 1, tn), lambda l, j: (l, 0, j)),
        ],
        out_specs=pl.BlockSpec((1, rows, tn), lambda l, j: (l, 0, j)),
        out_shape=jax.ShapeDtypeStruct((depth, rows, n), F32),
        compiler_params=_cparams("parallel", "parallel"),
        name="ada_mod",
    )(c_pad, ada_w, ada_b.reshape(depth, 1, n))


def _rms_rope(h, nw, cos, sin, low_half):
    hn = h * lax.rsqrt(jnp.mean(h * h, axis=-1, keepdims=True) + EPS) * nw
    swapped = jnp.where(low_half, pltpu.roll(hn, 96, 1), pltpu.roll(hn, 32, 1))
    return hn * cos + swapped * sin


def _in_proj_kernel(x_ref, mod_ref, nw_ref, w_ref, qnw_ref, knw_ref, cos_ref, sin_ref,
                    q_ref, k_ref, vt_ref, u_ref, ga_ref, gp_ref, *, d_model, tk):
    x = x_ref[0]
    xn = x * lax.rsqrt(jnp.mean(x * x, axis=-1, keepdims=True) + EPS) * nw_ref[...]
    shift = mod_ref[0, 0:1, :]
    scale = mod_ref[0, 1:2, :]
    xm = (xn * (1.0 + scale) + shift).astype(BF16)
    proj = jnp.dot(xm, w_ref[...], preferred_element_type=F32)

    cos = cos_ref[...]
    sin = sin_ref[...]
    lane = lax.broadcasted_iota(jnp.int32, cos.shape, 1)
    low_half = (lane % ROPE_HALF) < (ROPE_HALF // 2)
    sm_scale = math.log2(math.e) / math.sqrt(HEAD_DIM)
    attn_w = N_HEADS * HEAD_DIM
    kv_w = N_KV_HEADS * HEAD_DIM
    for h in range(N_HEADS):
        qh = _rms_rope(proj[:, h * HEAD_DIM:(h + 1) * HEAD_DIM], qnw_ref[...], cos, sin, low_half)
        q_ref[0, :, h * HEAD_DIM:(h + 1) * HEAD_DIM] = (qh * sm_scale).astype(BF16)
    for h in range(N_KV_HEADS):
        c0 = attn_w + h * HEAD_DIM
        kh = _rms_rope(proj[:, c0:c0 + HEAD_DIM], knw_ref[...], cos, sin, low_half)
        k_ref[0, :, h * HEAD_DIM:(h + 1) * HEAD_DIM] = kh.astype(BF16)
    c0 = attn_w + kv_w
    v_t = proj[:, c0:c0 + kv_w].T.astype(BF16)
    for h in range(N_KV_HEADS):
        for cj in range(v_t.shape[1] // tk):
            vt_ref[0, h, cj] = v_t[h * HEAD_DIM:(h + 1) * HEAD_DIM, cj * tk:(cj + 1) * tk]
    c0 += kv_w
    pool_w = u_ref.shape[-1]
    u_ref[0] = proj[:, c0:c0 + pool_w]
    c0 += pool_w
    ga_ref[0] = jax.nn.sigmoid(proj[:, c0:c0 + d_model]).astype(BF16)
    c0 += d_model
    gp_ref[0] = jax.nn.sigmoid(proj[:, c0:c0 + d_model]).astype(BF16)


def _in_proj(x, mod, nw, w_bf, qnw, knw, cos_t, sin_t, *, tile, tk):
    b, s, d = x.shape
    assert tile % tk == 0
    in_cols = w_bf.shape[1]
    attn_w = N_HEADS * HEAD_DIM
    kv_w = N_KV_HEADS * HEAD_DIM
    pool_w = in_cols - attn_w - 2 * kv_w - 2 * d
    tok = lambda width: pl.BlockSpec((1, tile, width), lambda bi, i: (bi, i, 0))
    full = lambda a: pl.BlockSpec(a.shape, lambda bi, i: (0,) * a.ndim)
    return pl.pallas_call(
        functools.partial(_in_proj_kernel, d_model=d, tk=tk),
        grid=(b, s // tile),
        in_specs=[
            tok(d),
            pl.BlockSpec((1, N_ADA, d), lambda bi, i: (bi, 0, 0)),
            full(nw), full(w_bf), full(qnw), full(knw),
            pl.BlockSpec((tile, HEAD_DIM), lambda bi, i: (i, 0)),
            pl.BlockSpec((tile, HEAD_DIM), lambda bi, i: (i, 0)),
        ],
        out_specs=[tok(attn_w), tok(kv_w),
                   pl.BlockSpec((1, N_KV_HEADS, tile // tk, HEAD_DIM, tk),
                                lambda bi, i: (bi, 0, i, 0, 0)),
                   tok(pool_w), tok(d), tok(d)],
        out_shape=[
            jax.ShapeDtypeStruct((b, s, attn_w), BF16),
            jax.ShapeDtypeStruct((b, s, kv_w), BF16),
            jax.ShapeDtypeStruct((b, N_KV_HEADS, s // tk, HEAD_DIM, tk), BF16),
            jax.ShapeDtypeStruct((b, s, pool_w), F32),
            jax.ShapeDtypeStruct((b, s, d), BF16),
            jax.ShapeDtypeStruct((b, s, d), BF16),
        ],
        compiler_params=_cparams("parallel", "parallel"),
        name="in_proj",
    )(x, mod, nw, w_bf, qnw, knw, cos_t, sin_t)


def _flash_kernel(q_ref, k_ref, vt_ref, o_ref, m_scr, l_scr, acc_scr, *, tq, tk, seq):
    q4 = jnp.concatenate(
        [q_ref[0, :, g * HEAD_DIM:(g + 1) * HEAD_DIM] for g in range(GROUP)], axis=0)
    m_scr[...] = jnp.full(m_scr.shape, -jnp.inf, F32)
    l_scr[...] = jnp.zeros(l_scr.shape, F32)
    acc_scr[...] = jnp.zeros(acc_scr.shape, F32)

    def body(j, carry):
        off = pl.multiple_of(j * tk, tk)
        kc = k_ref[0, pl.ds(off, tk), :]
        s_t = _nt_dot(kc, q4)
        m_prev = m_scr[...]
        m_new = jnp.maximum(m_prev, jnp.max(s_t, axis=0, keepdims=True))
        alpha = jnp.exp2(m_prev - m_new)
        p_t = jnp.exp2(s_t - m_new)
        l_scr[...] = alpha * l_scr[...] + jnp.sum(p_t, axis=0, keepdims=True)
        acc_scr[...] = alpha * acc_scr[...] + jnp.dot(
            vt_ref[0, 0, j], p_t.astype(BF16), preferred_element_type=F32)
        m_scr[...] = m_new
        return carry

    lax.fori_loop(0, seq // tk, body, 0)
    out = (acc_scr[...] / l_scr[...]).T
    for g in range(GROUP):
        o_ref[0, :, g * HEAD_DIM:(g + 1) * HEAD_DIM] = out[g * tq:(g + 1) * tq].astype(BF16)


def _flash_attention(q, k, vt, *, tq, tk):
    b, s, _ = q.shape
    gw = GROUP * HEAD_DIM
    return pl.pallas_call(
        functools.partial(_flash_kernel, tq=tq, tk=tk, seq=s),
        grid=(b, N_KV_HEADS, s // tq),
        in_specs=[
            pl.BlockSpec((1, tq, gw), lambda bi, kv, i: (bi, i, kv)),
            pl.BlockSpec((1, s, HEAD_DIM), lambda bi, kv, i: (bi, 0, kv)),
            pl.BlockSpec((1, 1, s // tk, HEAD_DIM, tk), lambda bi, kv, i: (bi, kv, 0, 0, 0)),
        ],
        out_specs=pl.BlockSpec((1, tq, gw), lambda bi, kv, i: (bi, i, kv)),
        out_shape=jax.ShapeDtypeStruct(q.shape, BF16),
        scratch_shapes=[
            pltpu.VMEM((1, GROUP * tq), F32),
            pltpu.VMEM((1, GROUP * tq), F32),
            pltpu.VMEM((HEAD_DIM, GROUP * tq), F32),
        ],
        compiler_params=_cparams("parallel", "parallel", "parallel"),
        name="flash_gqa",
    )(q, k, vt)


def _mix_kernel(attn_ref, u_ref, up_ref, un_ref, ga_ref, gp_ref, x_ref, mod_ref,
                wao_ref, pmw_ref, ps_ref, wpo_ref, wout_ref, o_ref, ext_scr, *, tile, seq):
    i = pl.program_id(1)
    nblk = pl.num_programs(1)
    pool_w = u_ref.shape[-1]
    prev_ok = (i > 0).astype(F32)
    next_ok = (i < nblk - 1).astype(F32)
    ext_scr[0:POOL_HALO, :] = up_ref[0] * prev_ok
    ext_scr[POOL_HALO:POOL_HALO + tile, :] = u_ref[0]
    ext_scr[POOL_HALO + tile:POOL_HALO + tile + POOL_HALO, :] = un_ref[0] * next_ok

    t = i * tile + lax.broadcasted_iota(jnp.int32, (tile, POOL_GROUP_DIM), 0)
    mixed = []
    for gi, w in enumerate(POOL_WINDOWS):
        cols = slice(gi * POOL_GROUP_DIM, (gi + 1) * POOL_GROUP_DIM)
        half = w // 2
        win = ext_scr[POOL_HALO - half:POOL_HALO - half + tile, cols]
        for kk in range(1, w):
            a = POOL_HALO - half + kk
            win = win + ext_scr[a:a + tile, cols]
        cnt = (jnp.minimum(t + half, seq) - jnp.maximum(t - half, 0)).astype(F32)
        m = win / cnt - u_ref[0, :, cols]
        mg = jnp.dot(m.astype(BF16), pmw_ref[gi], preferred_element_type=F32)
        mixed.append((mg * ps_ref[:, cols]).astype(BF16))
    pool_in = jnp.concatenate(mixed, axis=1)
    pool_o = jnp.dot(pool_in, wpo_ref[...], preferred_element_type=F32)
    attn_o = jnp.dot(attn_ref[0], wao_ref[...], preferred_element_type=F32)
    merged = ga_ref[0].astype(F32) * attn_o + gp_ref[0].astype(F32) * pool_o
    y = jnp.dot(merged.astype(BF16), wout_ref[...], preferred_element_type=F32)
    g1 = mod_ref[0, 2:3, :]
    o_ref[0] = x_ref[0] + g1 * y
    del pool_w


def _mix_out(attn, u, sga, sgp, x, mod, wao, pmw, ps, wpo, wout, *, tile):
    b, s, d = x.shape
    pool_w = u.shape[-1]
    hb = tile // POOL_HALO
    n_halo = s // POOL_HALO
    tok = lambda width: pl.BlockSpec((1, tile, width), lambda bi, i: (bi, i, 0))
    full = lambda a: pl.BlockSpec(a.shape, lambda bi, i: (0,) * a.ndim)
    return pl.pallas_call(
        functools.partial(_mix_kernel, tile=tile, seq=s),
        grid=(b, s // tile),
        in_specs=[
            tok(d), tok(pool_w),
            pl.BlockSpec((1, POOL_HALO, pool_w),
                         lambda bi, i: (bi, jnp.maximum(i * hb - 1, 0), 0)),
            pl.BlockSpec((1, POOL_HALO, pool_w),
                         lambda bi, i: (bi, jnp.minimum((i + 1) * hb, n_halo - 1), 0)),
            tok(d), tok(d), tok(d),
            pl.BlockSpec((1, N_ADA, d), lambda bi, i: (bi, 0, 0)),
            full(wao), full(pmw), full(ps), full(wpo), full(wout),
        ],
        out_specs=tok(d),
        out_shape=jax.ShapeDtypeStruct(x.shape, F32),
        scratch_shapes=[pltpu.VMEM((tile + 2 * POOL_HALO, pool_w), F32)],
        compiler_params=_cparams("parallel", "parallel"),
        name="mix_out",
    )(attn, u, u, u, sga, sgp, x, mod, wao, pmw, ps, wpo, wout)


_CAND_B_COUNTS = (16, 8, 8, 8, 8, 8, 8, 8)


def _top16_rows(s, row_iota, record_rank):
    vals, idxs = [], []
    rank = jnp.full(s.shape, NOT_SELECTED_RANK, F32)
    work = s
    big = float(s.shape[0])
    for r in range(PEER_TOPK):
        m = jnp.max(work, axis=0, keepdims=True)
        pos = jnp.min(jnp.where(work == m, row_iota, big), axis=0, keepdims=True)
        sel = row_iota == pos
        vals.append(m)
        idxs.append(pos)
        if record_rank:
            rank = jnp.where(sel, float(r), rank)
        work = jnp.where(sel, -jnp.inf, work)
    return jnp.concatenate(vals, axis=0), jnp.concatenate(idxs, axis=0), rank


def _route_kernel(x_ref, mod_ref, nw_ref, wq_ref, sk_ref,
                  xf_ref, a0_ref, n0_ref, a1_ref, r1_ref, *, tile):
    x = x_ref[0]
    xn = x * lax.rsqrt(jnp.mean(x * x, axis=-1, keepdims=True) + EPS) * nw_ref[...]
    shift = mod_ref[0, 3:4, :]
    scale = mod_ref[0, 4:5, :]
    xf = (xn * (1.0 + scale) + shift).astype(BF16)
    xf_ref[0] = xf
    q_t = _nt_dot(wq_ref[...], xf)

    key_iota = lax.broadcasted_iota(jnp.int32, (N_KEYS, tile), 0).astype(F32)
    n_cand = sum(_CAND_B_COUNTS) + (PEER_TOPK - len(_CAND_B_COUNTS))
    cand_iota = lax.broadcasted_iota(jnp.int32, (n_cand, tile), 0).astype(F32)

    for h in range(N_PEER_HEADS):
        scores = []
        for p in range(2):
            r0 = (h * 2 + p) * SUBKEY_DIM
            qs = q_t[r0:r0 + SUBKEY_DIM, :].astype(BF16)
            scores.append(jnp.dot(sk_ref[h, p], qs, preferred_element_type=F32))
        v0, i0, rank0 = _top16_rows(scores[0], key_iota, True)
        v1, _, rank1 = _top16_rows(scores[1], key_iota, True)
        del i0

        blocks = []
        for a, nb in enumerate(_CAND_B_COUNTS):
            blocks.append(v0[a:a + 1, :] + v1[0:nb, :])
        blocks.append(v0[len(_CAND_B_COUNTS):, :] + v1[0:1, :])
        cand = jnp.concatenate(blocks, axis=0)
        top = cand[0:1, :]
        work = cand
        selected = jnp.zeros(cand.shape, F32)
        for _ in range(PEER_TOPK):
            m = jnp.max(work, axis=0, keepdims=True)
            pos = jnp.min(jnp.where(work == m, cand_iota, float(n_cand)), axis=0, keepdims=True)
            sel = cand_iota == pos
            selected = jnp.where(sel, 1.0, selected)
            work = jnp.where(sel, -jnp.inf, work)
        z = jnp.sum(selected * jnp.exp(cand - top), axis=0, keepdims=True)
        log_z = jnp.log(z)

        n0 = jnp.zeros((N_KEYS, tile), F32)
        row = 0
        for a, nb in enumerate(_CAND_B_COUNTS):
            n_a = jnp.sum(selected[row:row + nb, :], axis=0, keepdims=True)
            n0 = jnp.where(rank0 == float(a), n_a, n0)
            row += nb
        for a in range(len(_CAND_B_COUNTS), PEER_TOPK):
            n0 = jnp.where(rank0 == float(a), selected[row:row + 1, :], n0)
            row += 1

        a0_ref[h] = scores[0] - v0[0:1, :] - log_z
        n0_ref[h] = n0
        a1_ref[h] = scores[1] - v1[0:1, :]
        r1_ref[h] = rank1


def _peer_route(x1, mod, nw, wq_t, sk_bf, *, tile):
    b, s, d = x1.shape
    n_tok = b * s
    nblk = s // tile
    dense = lambda: pl.BlockSpec((N_PEER_HEADS, N_KEYS, tile), lambda bi, i: (0, 0, bi * nblk + i))
    full = lambda a: pl.BlockSpec(a.shape, lambda bi, i: (0,) * a.ndim)
    dense_shape = jax.ShapeDtypeStruct((N_PEER_HEADS, N_KEYS, n_tok), F32)
    return pl.pallas_call(
        functools.partial(_route_kernel, tile=tile),
        grid=(b, nblk),
        in_specs=[
            pl.BlockSpec((1, tile, d), lambda bi, i: (bi, i, 0)),
            pl.BlockSpec((1, N_ADA, d), lambda bi, i: (bi, 0, 0)),
            full(nw), full(wq_t), full(sk_bf),
        ],
        out_specs=[pl.BlockSpec((1, tile, d), lambda bi, i: (bi, i, 0)),
                   dense(), dense(), dense(), dense()],
        out_shape=[jax.ShapeDtypeStruct((b, s, d), BF16),
                   dense_shape, dense_shape, dense_shape, dense_shape],
        compiler_params=_cparams("parallel", "parallel"),
        name="peer_route",
    )(x1, mod, nw, wq_t, sk_bf)


def _expert_kernel(xf_ref, a0_ref, n0_ref, a1_ref, r1_ref, u_ref, vt_ref, x_ref, mod_ref, fnw_ref,
                   o_ref, h_scr, act_scr, y_scr, *, tile, keys_per_chunk, final_norm):
    ec = pl.program_id(2)
    n_chunks = pl.num_programs(2)

    @pl.when(ec == 0)
    def _():
        y_scr[...] = jnp.zeros(y_scr.shape, F32)

    h_scr[...] = _nt_dot(u_ref[...], xf_ref[0])

    for ii in range(keys_per_chunk):
        key = ec * keys_per_chunk + ii
        gate = jnp.zeros((N_KEYS, tile), F32)
        for h in range(N_PEER_HEADS):
            a0 = a0_ref[h, pl.ds(key, 1), :]
            n0 = n0_ref[h, pl.ds(key, 1), :]
            logit = a0 + a1_ref[h]
            gate = gate + jnp.where(r1_ref[h] < n0, jnp.exp(logit), 0.0)
        hid = h_scr[ii * N_KEYS:(ii + 1) * N_KEYS, :]
        act = 0.5 * hid * (1.0 + lax.erf(hid * (1.0 / math.sqrt(2.0))))
        act_scr[ii * N_KEYS:(ii + 1) * N_KEYS, :] = (act * gate).astype(BF16)

    y_scr[...] += jnp.dot(vt_ref[...], act_scr[...], preferred_element_type=F32)

    @pl.when(ec == n_chunks - 1)
    def _():
        g2 = mod_ref[0, 5:6, :]
        out = x_ref[0] + g2 * y_scr[...].T
        if final_norm:
            out = out * lax.rsqrt(jnp.mean(out * out, axis=-1, keepdims=True) + EPS) * fnw_ref[...]
        o_ref[0] = out


def _peer_experts(xf, a0, n0, a1, r1, u_bf, vt_bf, x1, mod, fnw, *, tile, keys_per_chunk,
                  final_norm):
    b, s, d = x1.shape
    nblk = s // tile
    n_exp = u_bf.shape[0]
    chunk = keys_per_chunk * N_KEYS
    dense = lambda: pl.BlockSpec((N_PEER_HEADS, N_KEYS, tile),
                                 lambda bi, i, e: (0, 0, bi * nblk + i))
    tok = lambda: pl.BlockSpec((1, tile, d), lambda bi, i, e: (bi, i, 0))
    return pl.pallas_call(
        functools.partial(_expert_kernel, tile=tile, keys_per_chunk=keys_per_chunk,
                          final_norm=final_norm),
        grid=(b, nblk, n_exp // chunk),
        in_specs=[
            tok(), dense(), dense(), dense(), dense(),
            pl.BlockSpec((chunk, d), lambda bi, i, e: (e, 0)),
            pl.BlockSpec((d, chunk), lambda bi, i, e: (0, e)),
            tok(),
            pl.BlockSpec((1, N_ADA, d), lambda bi, i, e: (bi, 0, 0)),
            pl.BlockSpec(fnw.shape, lambda bi, i, e: (0, 0)),
        ],
        out_specs=tok(),
        out_shape=jax.ShapeDtypeStruct(x1.shape, F32),
        scratch_shapes=[
            pltpu.VMEM((chunk, tile), F32),
            pltpu.VMEM((chunk, tile), BF16),
            pltpu.VMEM((d, tile), F32),
        ],
        compiler_params=_cparams("parallel", "parallel", "arbitrary"),
        name="peer_experts",
    )(xf, a0, n0, a1, r1, u_bf, vt_bf, x1, mod, fnw)


def _rope_tables(seq):
    t = jnp.arange(seq)
    row = (t // GRID_W).astype(F32)
    col = (t % GRID_W).astype(F32)
    inv = 1.0 / (ROPE_THETA ** (jnp.arange(0, ROPE_HALF, 2, dtype=F32) / ROPE_HALF))
    ang_r = row[:, None] * inv
    ang_c = col[:, None] * inv
    cos_t = jnp.concatenate([jnp.cos(ang_r), jnp.cos(ang_r), jnp.cos(ang_c), jnp.cos(ang_c)], axis=1)
    sin_t = jnp.concatenate([-jnp.sin(ang_r), jnp.sin(ang_r), -jnp.sin(ang_c), jnp.sin(ang_c)], axis=1)
    return cos_t, sin_t


def _pick_tile(seq, want):
    tile = min(want, seq)
    assert seq % tile == 0
    return tile


def kernel(x, c, ada_w, ada_b, norm_mix_w, w_in, q_norm_w, k_norm_w, w_attn_o, pool_mix_w, pool_scale, w_pool_o, w_out, norm_ffn_w, w_query, sub_keys, expert_u, expert_v, final_norm_w):
    b, s, d = x.shape
    depth = ada_w.shape[0]
    cos_t, sin_t = _rope_tables(s)

    rows = 8
    c_pad = jnp.zeros((rows, d), F32).at[:b].set(c)
    mod_all = _ada_mod(c_pad, ada_w, ada_b)[:, :b].reshape(depth, b, N_ADA, d)

    t_proj = _pick_tile(s, 512)
    t_q = _pick_tile(s, 256)
    t_k = _pick_tile(s, 512)
    t_mix = _pick_tile(s, 512)
    t_route = _pick_tile(s, 256)
    t_exp = _pick_tile(s, 512)

    for l in range(depth):
        mod = mod_all[l]
        q, k, vt, u, sga, sgp = _in_proj(
            x, mod, norm_mix_w[l][None], w_in[l].astype(BF16), q_norm_w[l][None],
            k_norm_w[l][None], cos_t, sin_t, tile=t_proj, tk=t_k)
        attn = _flash_attention(q, k, vt, tq=t_q, tk=t_k)
        x1 = _mix_out(attn, u, sga, sgp, x, mod, w_attn_o[l].astype(BF16),
                      pool_mix_w[l].astype(BF16), pool_scale[l][None],
                      w_pool_o[l].astype(BF16), w_out[l].astype(BF16), tile=t_mix)
        xf, a0, n0, a1, r1 = _peer_route(
            x1, mod, norm_ffn_w[l][None], w_query[l].T.astype(BF16),
            sub_keys[l].astype(BF16), tile=t_route)
        x = _peer_experts(
            xf, a0, n0, a1, r1, expert_u[l].astype(BF16), expert_v[l].T.astype(BF16), x1, mod,
            final_norm_w[None], tile=t_exp, keys_per_chunk=8, final_norm=(l == depth - 1))
    return x
```

```python
import functools
import math

import jax
import jax.numpy as jnp
import numpy as np
from jax import lax
from jax.experimental import pallas as pl
from jax.experimental.pallas import tpu as pltpu

F32 = jnp.float32
BF16 = jnp.bfloat16

EPS = 1e-6
GRID_W = 64
HEAD_DIM = 128
N_HEADS = 8
N_KV_HEADS = 2
GROUP = N_HEADS // N_KV_HEADS
ROPE_THETA = 10000.0
ROPE_HALF = HEAD_DIM // 2
POOL_WINDOWS = (2, 4, 8, 16)
POOL_GROUP_DIM = 128
POOL_HALO = 8
N_PEER_HEADS = 8
SUBKEY_DIM = 256
N_KEYS = 128
PEER_TOPK = 16
N_ADA = 6
NOT_SELECTED_RANK = 99.0

VMEM_LIMIT_BYTES = 56 * 1024 * 1024
ATTN_SCORE_VMEM_BYTES = 48 * 1024 * 1024


def _cparams(*sem):
    return pltpu.CompilerParams(dimension_semantics=sem, vmem_limit_bytes=VMEM_LIMIT_BYTES)


def _nt_dot(a, b):
    return lax.dot_general(a, b, (((1,), (1,)), ((), ())), preferred_element_type=F32)


def _ada_kernel(c_ref, w_ref, b_ref, o_ref):
    c = c_ref[...]
    c_act = c * jax.nn.sigmoid(c)
    o_ref[0] = jnp.dot(c_act, w_ref[0], preferred_element_type=F32,
                       precision=lax.Precision.HIGHEST) + b_ref[0]


def _ada_mod(c_pad, ada_w, ada_b):
    depth, d, n = ada_w.shape
    rows = c_pad.shape[0]
    tn = 1536
    return pl.pallas_call(
        _ada_kernel,
        grid=(depth, n // tn),
        in_specs=[
            pl.BlockSpec((rows, d), lambda l, j: (0, 0)),
            pl.BlockSpec((1, d, tn), lambda l, j: (l, 0, j)),
            pl.BlockSpec((1, 1, tn), lambda l, j: (l, 0, j)),
        ],
        out_specs=pl.BlockSpec((1, rows, tn), lambda l, j: (l, 0, j)),
        out_shape=jax.ShapeDtypeStruct((depth, rows, n), F32),
        compiler_params=_cparams("parallel", "parallel"),
        name="ada_mod",
    )(c_pad, ada_w, ada_b.reshape(depth, 1, n))


def _rms_rope(h, nw, cos, sin, low_half):
    hn = h * lax.rsqrt(jnp.mean(h * h, axis=-1, keepdims=True) + EPS) * nw
    swapped = jnp.where(low_half, pltpu.roll(hn, 96, 1), pltpu.roll(hn, 32, 1))
    return hn * cos + swapped * sin


def _in_proj_kernel(x_ref, mod_ref, nw_ref, w_ref, qnw_ref, knw_ref, cos_ref, sin_ref,
                    q_ref, k_ref, vt_ref, u_ref, ga_ref, gp_ref, *, d_model, tk):
    x = x_ref[0]
    xn = x * lax.rsqrt(jnp.mean(x * x, axis=-1, keepdims=True) + EPS) * nw_ref[...]
    shift = mod_ref[0, 0:1, :]
    scale = mod_ref[0, 1:2, :]
    xm = (xn * (1.0 + scale) + shift).astype(BF16)
    proj = jnp.dot(xm, w_ref[...], preferred_element_type=F32)

    cos = cos_ref[...]
    sin = sin_ref[...]
    lane = lax.broadcasted_iota(jnp.int32, cos.shape, 1)
    low_half = (lane % ROPE_HALF) < (ROPE_HALF // 2)
    sm_scale = math.log2(math.e) / math.sqrt(HEAD_DIM)
    attn_w = N_HEADS * HEAD_DIM
    kv_w = N_KV_HEADS * HEAD_DIM
    for h in range(N_HEADS):
        qh = _rms_rope(proj[:, h * HEAD_DIM:(h + 1) * HEAD_DIM], qnw_ref[...], cos, sin, low_half)
        q_ref[0, :, h * HEAD_DIM:(h + 1) * HEAD_DIM] = (qh * sm_scale).astype(BF16)
    for h in range(N_KV_HEADS):
        c0 = attn_w + h * HEAD_DIM
        kh = _rms_rope(proj[:, c0:c0 + HEAD_DIM], knw_ref[...], cos, sin, low_half)
        k_ref[0, :, h * HEAD_DIM:(h + 1) * HEAD_DIM] = kh.astype(BF16)
    c0 = attn_w + kv_w
    v_t = proj[:, c0:c0 + kv_w].T.astype(BF16)
    vw = vt_ref.shape[-1]
    for h in range(N_KV_HEADS):
        for cj in range(v_t.shape[1] // vw):
            vt_ref[0, h, cj] = v_t[h * HEAD_DIM:(h + 1) * HEAD_DIM, cj * vw:(cj + 1) * vw]
    del tk
    c0 += kv_w
    pool_w = u_ref.shape[-1]
    u_ref[0] = proj[:, c0:c0 + pool_w]
    c0 += pool_w
    ga_ref[0] = jax.nn.sigmoid(proj[:, c0:c0 + d_model]).astype(BF16)
    c0 += d_model
    gp_ref[0] = jax.nn.sigmoid(proj[:, c0:c0 + d_model]).astype(BF16)


def _in_proj(x, mod, nw, w_bf, qnw, knw, cos_t, sin_t, *, tile, tk):
    b, s, d = x.shape
    if tile >= tk:
        assert tile % tk == 0
        vt_spec = pl.BlockSpec((1, N_KV_HEADS, tile // tk, HEAD_DIM, tk),
                               lambda bi, i: (bi, 0, i, 0, 0))
    else:
        assert tk % tile == 0
        per = tk // tile
        vt_spec = pl.BlockSpec((1, N_KV_HEADS, 1, HEAD_DIM, tile),
                               lambda bi, i: (bi, 0, i // per, 0, i % per))
    in_cols = w_bf.shape[1]
    attn_w = N_HEADS * HEAD_DIM
    kv_w = N_KV_HEADS * HEAD_DIM
    pool_w = in_cols - attn_w - 2 * kv_w - 2 * d
    tok = lambda width: pl.BlockSpec((1, tile, width), lambda bi, i: (bi, i, 0))
    full = lambda a: pl.BlockSpec(a.shape, lambda bi, i: (0,) * a.ndim)
    return pl.pallas_call(
        functools.partial(_in_proj_kernel, d_model=d, tk=tk),
        grid=(b, s // tile),
        in_specs=[
            tok(d),
            pl.BlockSpec((1, N_ADA, d), lambda bi, i: (bi, 0, 0)),
            full(nw), full(w_bf), full(qnw), full(knw),
            pl.BlockSpec((tile, HEAD_DIM), lambda bi, i: (i, 0)),
            pl.BlockSpec((tile, HEAD_DIM), lambda bi, i: (i, 0)),
        ],
        out_specs=[tok(attn_w), tok(kv_w), vt_spec, tok(pool_w), tok(d), tok(d)],
        out_shape=[
            jax.ShapeDtypeStruct((b, s, attn_w), BF16),
            jax.ShapeDtypeStruct((b, s, kv_w), BF16),
            jax.ShapeDtypeStruct((b, N_KV_HEADS, s // tk, HEAD_DIM, tk), BF16),
            jax.ShapeDtypeStruct((b, s, pool_w), F32),
            jax.ShapeDtypeStruct((b, s, d), BF16),
            jax.ShapeDtypeStruct((b, s, d), BF16),
        ],
        compiler_params=_cparams("parallel", "parallel"),
        name="in_proj",
    )(x, mod, nw, w_bf, qnw, knw, cos_t, sin_t)


def _flash_kernel(q_ref, k_ref, vt_ref, o_ref, m_scr, l_scr, acc_scr, *, tq, tk, seq):
    q4 = jnp.concatenate(
        [q_ref[0, :, g * HEAD_DIM:(g + 1) * HEAD_DIM] for g in range(GROUP)], axis=0)

    def write_out(acc, l):
        out = (acc / l).T
        for g in range(GROUP):
            o_ref[0, :, g * HEAD_DIM:(g + 1) * HEAD_DIM] = out[g * tq:(g + 1) * tq].astype(BF16)

    if tk == seq:
        s_t = _nt_dot(k_ref[0], q4)
        p_t = jnp.exp2(s_t - jnp.max(s_t, axis=0, keepdims=True))
        l = jnp.sum(p_t, axis=0, keepdims=True)
        acc = jnp.dot(vt_ref[0, 0, 0], p_t.astype(BF16), preferred_element_type=F32)
        write_out(acc, l)
        return

    m_scr[...] = jnp.full(m_scr.shape, -jnp.inf, F32)
    l_scr[...] = jnp.zeros(l_scr.shape, F32)
    acc_scr[...] = jnp.zeros(acc_scr.shape, F32)

    def body(j, carry):
        off = pl.multiple_of(j * tk, tk)
        kc = k_ref[0, pl.ds(off, tk), :]
        s_t = _nt_dot(kc, q4)
        m_prev = m_scr[...]
        m_new = jnp.maximum(m_prev, jnp.max(s_t, axis=0, keepdims=True))
        alpha = jnp.exp2(m_prev - m_new)
        p_t = jnp.exp2(s_t - m_new)
        l_scr[...] = alpha * l_scr[...] + jnp.sum(p_t, axis=0, keepdims=True)
        acc_scr[...] = alpha * acc_scr[...] + jnp.dot(
            vt_ref[0, 0, j], p_t.astype(BF16), preferred_element_type=F32)
        m_scr[...] = m_new
        return carry

    lax.fori_loop(0, seq // tk, body, 0)
    write_out(acc_scr[...], l_scr[...])


def _flash_attention(q, k, vt, *, tq, tk):
    b, s, _ = q.shape
    gw = GROUP * HEAD_DIM
    return pl.pallas_call(
        functools.partial(_flash_kernel, tq=tq, tk=tk, seq=s),
        grid=(b, N_KV_HEADS, s // tq),
        in_specs=[
            pl.BlockSpec((1, tq, gw), lambda bi, kv, i: (bi, i, kv)),
            pl.BlockSpec((1, s, HEAD_DIM), lambda bi, kv, i: (bi, 0, kv)),
            pl.BlockSpec((1, 1, s // tk, HEAD_DIM, tk), lambda bi, kv, i: (bi, kv, 0, 0, 0)),
        ],
        out_specs=pl.BlockSpec((1, tq, gw), lambda bi, kv, i: (bi, i, kv)),
        out_shape=jax.ShapeDtypeStruct(q.shape, BF16),
        scratch_shapes=[
            pltpu.VMEM((1, GROUP * tq), F32),
            pltpu.VMEM((1, GROUP * tq), F32),
            pltpu.VMEM((HEAD_DIM, GROUP * tq), F32),
        ],
        compiler_params=_cparams("parallel", "parallel", "parallel"),
        name="flash_gqa",
    )(q, k, vt)


def _mix_kernel(attn_ref, u_ref, up_ref, un_ref, ga_ref, gp_ref, x_ref, mod_ref,
                wao_ref, pmw_ref, ps_ref, wpo_ref, wout_ref, o_ref, ext_scr, *, tile, seq):
    i = pl.program_id(1)
    nblk = pl.num_programs(1)
    pool_w = u_ref.shape[-1]
    prev_ok = (i > 0).astype(F32)
    next_ok = (i < nblk - 1).astype(F32)
    ext_scr[0:POOL_HALO, :] = up_ref[0] * prev_ok
    ext_scr[POOL_HALO:POOL_HALO + tile, :] = u_ref[0]
    ext_scr[POOL_HALO + tile:POOL_HALO + tile + POOL_HALO, :] = un_ref[0] * next_ok

    t = i * tile + lax.broadcasted_iota(jnp.int32, (tile, POOL_GROUP_DIM), 0)
    mixed = []
    for gi, w in enumerate(POOL_WINDOWS):
        cols = slice(gi * POOL_GROUP_DIM, (gi + 1) * POOL_GROUP_DIM)
        half = w // 2
        win = ext_scr[POOL_HALO - half:POOL_HALO - half + tile, cols]
        for kk in range(1, w):
            a = POOL_HALO - half + kk
            win = win + ext_scr[a:a + tile, cols]
        cnt = (jnp.minimum(t + half, seq) - jnp.maximum(t - half, 0)).astype(F32)
        m = win / cnt - u_ref[0, :, cols]
        mg = jnp.dot(m.astype(BF16), pmw_ref[gi], preferred_element_type=F32)
        mixed.append((mg * ps_ref[:, cols]).astype(BF16))
    pool_in = jnp.concatenate(mixed, axis=1)
    pool_o = jnp.dot(pool_in, wpo_ref[...], preferred_element_type=F32)
    attn_o = jnp.dot(attn_ref[0], wao_ref[...], preferred_element_type=F32)
    merged = ga_ref[0].astype(F32) * attn_o + gp_ref[0].astype(F32) * pool_o
    y = jnp.dot(merged.astype(BF16), wout_ref[...], preferred_element_type=F32)
    g1 = mod_ref[0, 2:3, :]
    o_ref[0] = x_ref[0] + g1 * y
    del pool_w


def _mix_out(attn, u, sga, sgp, x, mod, wao, pmw, ps, wpo, wout, *, tile):
    b, s, d = x.shape
    pool_w = u.shape[-1]
    hb = tile // POOL_HALO
    n_halo = s // POOL_HALO
    tok = lambda width: pl.BlockSpec((1, tile, width), lambda bi, i: (bi, i, 0))
    full = lambda a: pl.BlockSpec(a.shape, lambda bi, i: (0,) * a.ndim)
    return pl.pallas_call(
        functools.partial(_mix_kernel, tile=tile, seq=s),
        grid=(b, s // tile),
        in_specs=[
            tok(d), tok(pool_w),
            pl.BlockSpec((1, POOL_HALO, pool_w),
                         lambda bi, i: (bi, jnp.maximum(i * hb - 1, 0), 0)),
            pl.BlockSpec((1, POOL_HALO, pool_w),
                         lambda bi, i: (bi, jnp.minimum((i + 1) * hb, n_halo - 1), 0)),
            tok(d), tok(d), tok(d),
            pl.BlockSpec((1, N_ADA, d), lambda bi, i: (bi, 0, 0)),
            full(wao), full(pmw), full(ps), full(wpo), full(wout),
        ],
        out_specs=tok(d),
        out_shape=jax.ShapeDtypeStruct(x.shape, F32),
        scratch_shapes=[pltpu.VMEM((tile + 2 * POOL_HALO, pool_w), F32)],
        compiler_params=_cparams("parallel", "parallel"),
        name="mix_out",
    )(attn, u, u, u, sga, sgp, x, mod, wao, pmw, ps, wpo, wout)


_CAND_B_COUNTS = (16, 8, 8, 8, 8, 8, 8, 8)


def _top16_rows(s, row_iota, record_rank):
    vals, idxs = [], []
    rank = jnp.full(s.shape, NOT_SELECTED_RANK, F32)
    work = s
    big = float(s.shape[0])
    for r in range(PEER_TOPK):
        m = jnp.max(work, axis=0, keepdims=True)
        pos = jnp.min(jnp.where(work == m, row_iota, big), axis=0, keepdims=True)
        sel = row_iota == pos
        vals.append(m)
        idxs.append(pos)
        if record_rank:
            rank = jnp.where(sel, float(r), rank)
        work = jnp.where(sel, -jnp.inf, work)
    return jnp.concatenate(vals, axis=0), jnp.concatenate(idxs, axis=0), rank


def _bf16_pair_bits(v):
    bits = pltpu.bitcast(v.astype(BF16).astype(F32), jnp.uint32)
    return bits | (bits >> 16)


def _route_kernel(x_ref, mod_ref, nw_ref, wq_ref, sk_ref,
                  xf_ref, e0_ref, n0_ref, e1_ref, r1_ref, *, tile):
    x = x_ref[0]
    xn = x * lax.rsqrt(jnp.mean(x * x, axis=-1, keepdims=True) + EPS) * nw_ref[...]
    shift = mod_ref[0, 3:4, :]
    scale = mod_ref[0, 4:5, :]
    xf = (xn * (1.0 + scale) + shift).astype(BF16)
    xf_ref[0] = xf
    q_t = _nt_dot(wq_ref[...], xf)

    key_iota = lax.broadcasted_iota(jnp.int32, (N_KEYS, tile), 0).astype(F32)
    n_cand = sum(_CAND_B_COUNTS) + (PEER_TOPK - len(_CAND_B_COUNTS))
    cand_iota = lax.broadcasted_iota(jnp.int32, (n_cand, tile), 0).astype(F32)

    for h in range(N_PEER_HEADS):
        scores = []
        for p in range(2):
            r0 = (h * 2 + p) * SUBKEY_DIM
            qs = q_t[r0:r0 + SUBKEY_DIM, :].astype(BF16)
            scores.append(jnp.dot(sk_ref[h, p], qs, preferred_element_type=F32))
        v0, i0, rank0 = _top16_rows(scores[0], key_iota, True)
        v1, _, rank1 = _top16_rows(scores[1], key_iota, True)
        del i0

        blocks = []
        for a, nb in enumerate(_CAND_B_COUNTS):
            blocks.append(v0[a:a + 1, :] + v1[0:nb, :])
        blocks.append(v0[len(_CAND_B_COUNTS):, :] + v1[0:1, :])
        cand = jnp.concatenate(blocks, axis=0)
        top = cand[0:1, :]
        work = cand
        selected = jnp.zeros(cand.shape, F32)
        for _ in range(PEER_TOPK):
            m = jnp.max(work, axis=0, keepdims=True)
            pos = jnp.min(jnp.where(work == m, cand_iota, float(n_cand)), axis=0, keepdims=True)
            sel = cand_iota == pos
            selected = jnp.where(sel, 1.0, selected)
            work = jnp.where(sel, -jnp.inf, work)
        z = jnp.sum(selected * jnp.exp(cand - top), axis=0, keepdims=True)

        n0 = jnp.zeros((N_KEYS, tile), F32)
        row = 0
        for a, nb in enumerate(_CAND_B_COUNTS):
            n_a = jnp.sum(selected[row:row + nb, :], axis=0, keepdims=True)
            n0 = jnp.where(rank0 == float(a), n_a, n0)
            row += nb
        for a in range(len(_CAND_B_COUNTS), PEER_TOPK):
            n0 = jnp.where(rank0 == float(a), selected[row:row + 1, :], n0)
            row += 1

        e0_ref[h] = _bf16_pair_bits(jnp.exp(scores[0] - v0[0:1, :]) / z)
        n0_ref[h] = _bf16_pair_bits(n0)
        e1_ref[h] = jnp.exp(scores[1] - v1[0:1, :]).astype(BF16)
        r1_ref[h] = rank1.astype(BF16)


def _peer_route(x1, mod, nw, wq_t, sk_bf, *, tile):
    b, s, d = x1.shape
    n_tok = b * s
    nblk = s // tile
    dense = lambda: pl.BlockSpec((N_PEER_HEADS, N_KEYS, tile), lambda bi, i: (0, 0, bi * nblk + i))
    full = lambda a: pl.BlockSpec(a.shape, lambda bi, i: (0,) * a.ndim)
    pair_shape = jax.ShapeDtypeStruct((N_PEER_HEADS, N_KEYS, n_tok), jnp.uint32)
    bf_shape = jax.ShapeDtypeStruct((N_PEER_HEADS, N_KEYS, n_tok), BF16)
    return pl.pallas_call(
        functools.partial(_route_kernel, tile=tile),
        grid=(b, nblk),
        in_specs=[
            pl.BlockSpec((1, tile, d), lambda bi, i: (bi, i, 0)),
            pl.BlockSpec((1, N_ADA, d), lambda bi, i: (bi, 0, 0)),
            full(nw), full(wq_t), full(sk_bf),
        ],
        out_specs=[pl.BlockSpec((1, tile, d), lambda bi, i: (bi, i, 0)),
                   dense(), dense(), dense(), dense()],
        out_shape=[jax.ShapeDtypeStruct((b, s, d), BF16),
                   pair_shape, pair_shape, bf_shape, bf_shape],
        compiler_params=_cparams("parallel", "parallel"),
        name="peer_route",
    )(x1, mod, nw, wq_t, sk_bf)


def _expert_kernel(xf_ref, e0_ref, n0_ref, e1_ref, r1_ref, u_ref, vt_ref, x_ref, mod_ref, fnw_ref,
                   o_ref, h_scr, act_scr, y_scr, *, tile, keys_per_chunk, final_norm):
    ec = pl.program_id(2)
    n_chunks = pl.num_programs(2)

    @pl.when(ec == 0)
    def _():
        y_scr[...] = jnp.zeros(y_scr.shape, F32)

    h_scr[...] = _nt_dot(u_ref[...], xf_ref[0])

    def row_bf16(ref, h, key):
        word = ref[h, pl.ds(key, 1), :]
        return pltpu.bitcast(jnp.broadcast_to(word, (N_KEYS // 2, tile)), BF16)

    for ii in range(keys_per_chunk):
        key = ec * keys_per_chunk + ii
        gate = jnp.zeros((N_KEYS, tile), BF16)
        for h in range(N_PEER_HEADS):
            e0 = row_bf16(e0_ref, h, key)
            n0 = row_bf16(n0_ref, h, key)
            gate = gate + jnp.where(r1_ref[h] < n0, e0 * e1_ref[h], jnp.zeros((), BF16))
        hid = h_scr[ii * N_KEYS:(ii + 1) * N_KEYS, :]
        act = 0.5 * hid * (1.0 + lax.erf(hid * (1.0 / math.sqrt(2.0))))
        act_scr[ii * N_KEYS:(ii + 1) * N_KEYS, :] = act.astype(BF16) * gate

    y_scr[...] += jnp.dot(vt_ref[...], act_scr[...], preferred_element_type=F32)

    @pl.when(ec == n_chunks - 1)
    def _():
        g2 = mod_ref[0, 5:6, :]
        out = x_ref[0] + g2 * y_scr[...].T
        if final_norm:
            out = out * lax.rsqrt(jnp.mean(out * out, axis=-1, keepdims=True) + EPS) * fnw_ref[...]
        o_ref[0] = out


def _peer_experts(xf, e0, n0, e1, r1, u_bf, vt_bf, x1, mod, fnw, *, tile, keys_per_chunk,
                  final_norm):
    b, s, d = x1.shape
    nblk = s // tile
    n_exp = u_bf.shape[0]
    chunk = keys_per_chunk * N_KEYS
    dense = lambda: pl.BlockSpec((N_PEER_HEADS, N_KEYS, tile),
                                 lambda bi, i, e: (0, 0, bi * nblk + i))
    tok = lambda: pl.BlockSpec((1, tile, d), lambda bi, i, e: (bi, i, 0))
    return pl.pallas_call(
        functools.partial(_expert_kernel, tile=tile, keys_per_chunk=keys_per_chunk,
                          final_norm=final_norm),
        grid=(b, nblk, n_exp // chunk),
        in_specs=[
            tok(), dense(), dense(), dense(), dense(),
            pl.BlockSpec((chunk, d), lambda bi, i, e: (e, 0)),
            pl.BlockSpec((d, chunk), lambda bi, i, e: (0, e)),
            tok(),
            pl.BlockSpec((1, N_ADA, d), lambda bi, i, e: (bi, 0, 0)),
            pl.BlockSpec(fnw.shape, lambda bi, i, e: (0, 0)),
        ],
        out_specs=tok(),
        out_shape=jax.ShapeDtypeStruct(x1.shape, F32),
        scratch_shapes=[
            pltpu.VMEM((chunk, tile), F32),
            pltpu.VMEM((chunk, tile), BF16),
            pltpu.VMEM((d, tile), F32),
        ],
        compiler_params=_cparams("parallel", "parallel", "arbitrary"),
        name="peer_experts",
    )(xf, e0, n0, e1, r1, u_bf, vt_bf, x1, mod, fnw)


def _rope_tables(seq):
    t = jnp.arange(seq)
    row = (t // GRID_W).astype(F32)
    col = (t % GRID_W).astype(F32)
    inv = 1.0 / (ROPE_THETA ** (jnp.arange(0, ROPE_HALF, 2, dtype=F32) / ROPE_HALF))
    ang_r = row[:, None] * inv
    ang_c = col[:, None] * inv
    cos_t = jnp.concatenate([jnp.cos(ang_r), jnp.cos(ang_r), jnp.cos(ang_c), jnp.cos(ang_c)], axis=1)
    sin_t = jnp.concatenate([-jnp.sin(ang_r), jnp.sin(ang_r), -jnp.sin(ang_c), jnp.sin(ang_c)], axis=1)
    return cos_t, sin_t


def _pick_tile(seq, want):
    tile = min(want, seq)
    assert seq % tile == 0
    return tile


def kernel(x, c, ada_w, ada_b, norm_mix_w, w_in, q_norm_w, k_norm_w, w_attn_o, pool_mix_w, pool_scale, w_pool_o, w_out, norm_ffn_w, w_query, sub_keys, expert_u, expert_v, final_norm_w):
    b, s, d = x.shape
    depth = ada_w.shape[0]
    cos_t, sin_t = _rope_tables(s)

    rows = 8
    c_pad = jnp.zeros((rows, d), F32).at[:b].set(c)
    mod_all = _ada_mod(c_pad, ada_w, ada_b)[:, :b].reshape(depth, b, N_ADA, d)

    t_proj = _pick_tile(s, 512)
    t_q = _pick_tile(s, 256)
    single_block_bytes = s * GROUP * t_q * 6
    t_k = s if single_block_bytes <= ATTN_SCORE_VMEM_BYTES else _pick_tile(s, 512)
    t_mix = _pick_tile(s, 512)
    t_route = _pick_tile(s, 256)
    t_exp = _pick_tile(s, 512)

    for l in range(depth):
        mod = mod_all[l]
        q, k, vt, u, sga, sgp = _in_proj(
            x, mod, norm_mix_w[l][None], w_in[l].astype(BF16), q_norm_w[l][None],
            k_norm_w[l][None], cos_t, sin_t, tile=t_proj, tk=t_k)
        attn = _flash_attention(q, k, vt, tq=t_q, tk=t_k)
        x1 = _mix_out(attn, u, sga, sgp, x, mod, w_attn_o[l].astype(BF16),
                      pool_mix_w[l].astype(BF16), pool_scale[l][None],
                      w_pool_o[l].astype(BF16), w_out[l].astype(BF16), tile=t_mix)
        xf, e0, n0, e1, r1 = _peer_route(
            x1, mod, norm_ffn_w[l][None], w_query[l].T.astype(BF16),
            sub_keys[l].astype(BF16), tile=t_route)
        x = _peer_experts(
            xf, e0, n0, e1, r1, expert_u[l].astype(BF16), expert_v[l].T.astype(BF16), x1, mod,
            final_norm_w[None], tile=t_exp, keys_per_chunk=8, final_norm=(l == depth - 1))
    return x
```

```python
import functools
import math

import jax
import jax.numpy as jnp
import numpy as np
from jax import lax
from jax.experimental import pallas as pl
from jax.experimental.pallas import tpu as pltpu

F32 = jnp.float32
BF16 = jnp.bfloat16

EPS = 1e-6
GRID_W = 64
HEAD_DIM = 128
N_HEADS = 8
N_KV_HEADS = 2
GROUP = N_HEADS // N_KV_HEADS
ROPE_THETA = 10000.0
ROPE_HALF = HEAD_DIM // 2
POOL_WINDOWS = (2, 4, 8, 16)
POOL_GROUP_DIM = 128
POOL_HALO = 8
N_PEER_HEADS = 8
SUBKEY_DIM = 256
N_KEYS = 128
PEER_TOPK = 16
N_ADA = 6
NOT_SELECTED_RANK = 99.0
KEYS_PER_PIECE = 2

VMEM_LIMIT_BYTES = 56 * 1024 * 1024
ATTN_SCORE_VMEM_BYTES = 48 * 1024 * 1024


def _cparams(*sem):
    return pltpu.CompilerParams(dimension_semantics=sem, vmem_limit_bytes=VMEM_LIMIT_BYTES)


def _nt_dot(a, b):
    return lax.dot_general(a, b, (((1,), (1,)), ((), ())), preferred_element_type=F32)


def _ada_kernel(c_ref, w_ref, b_ref, o_ref):
    c = c_ref[...]
    c_act = c * jax.nn.sigmoid(c)
    o_ref[0] = jnp.dot(c_act, w_ref[0], preferred_element_type=F32,
                       precision=lax.Precision.HIGHEST) + b_ref[0]


def _ada_mod(c_pad, ada_w, ada_b):
    depth, d, n = ada_w.shape
    rows = c_pad.shape[0]
    tn = 1536
    return pl.pallas_call(
        _ada_kernel,
        grid=(depth, n // tn),
        in_specs=[
            pl.BlockSpec((rows, d), lambda l, j: (0, 0)),
            pl.BlockSpec((1, d, tn), lambda l, j: (l, 0, j)),
            pl.BlockSpec((1, 1, tn), lambda l, j: (l, 0, j)),
        ],
        out_specs=pl.BlockSpec((1, rows, tn), lambda l, j: (l, 0, j)),
        out_shape=jax.ShapeDtypeStruct((depth, rows, n), F32),
        compiler_params=_cparams("parallel", "parallel"),
        name="ada_mod",
    )(c_pad, ada_w, ada_b.reshape(depth, 1, n))


def _rms_rope(h, nw, cos, sin, low_half):
    hn = h * lax.rsqrt(jnp.mean(h * h, axis=-1, keepdims=True) + EPS) * nw
    swapped = jnp.where(low_half, pltpu.roll(hn, 96, 1), pltpu.roll(hn, 32, 1))
    return hn * cos + swapped * sin


def _in_proj_kernel(x_ref, mod_ref, nw_ref, w_ref, qnw_ref, knw_ref, cos_ref, sin_ref,
                    q_ref, k_ref, vt_ref, u_ref, ga_ref, gp_ref, *, d_model, tk):
    x = x_ref[0]
    xn = x * lax.rsqrt(jnp.mean(x * x, axis=-1, keepdims=True) + EPS) * nw_ref[...]
    shift = mod_ref[0, 0:1, :]
    scale = mod_ref[0, 1:2, :]
    xm = (xn * (1.0 + scale) + shift).astype(BF16)
    proj = jnp.dot(xm, w_ref[...], preferred_element_type=F32)

    cos = cos_ref[...]
    sin = sin_ref[...]
    lane = lax.broadcasted_iota(jnp.int32, cos.shape, 1)
    low_half = (lane % ROPE_HALF) < (ROPE_HALF // 2)
    sm_scale = math.log2(math.e) / math.sqrt(HEAD_DIM)
    attn_w = N_HEADS * HEAD_DIM
    kv_w = N_KV_HEADS * HEAD_DIM
    for h in range(N_HEADS):
        qh = _rms_rope(proj[:, h * HEAD_DIM:(h + 1) * HEAD_DIM], qnw_ref[...], cos, sin, low_half)
        q_ref[0, :, h * HEAD_DIM:(h + 1) * HEAD_DIM] = (qh * sm_scale).astype(BF16)
    for h in range(N_KV_HEADS):
        c0 = attn_w + h * HEAD_DIM
        kh = _rms_rope(proj[:, c0:c0 + HEAD_DIM], knw_ref[...], cos, sin, low_half)
        k_ref[0, :, h * HEAD_DIM:(h + 1) * HEAD_DIM] = kh.astype(BF16)
    c0 = attn_w + kv_w
    v_t = proj[:, c0:c0 + kv_w].T.astype(BF16)
    vw = vt_ref.shape[-1]
    for h in range(N_KV_HEADS):
        for cj in range(v_t.shape[1] // vw):
            vt_ref[0, h, cj] = v_t[h * HEAD_DIM:(h + 1) * HEAD_DIM, cj * vw:(cj + 1) * vw]
    del tk
    c0 += kv_w
    pool_w = u_ref.shape[-1]
    u_ref[0] = proj[:, c0:c0 + pool_w]
    c0 += pool_w
    ga_ref[0] = jax.nn.sigmoid(proj[:, c0:c0 + d_model]).astype(BF16)
    c0 += d_model
    gp_ref[0] = jax.nn.sigmoid(proj[:, c0:c0 + d_model]).astype(BF16)


def _in_proj(x, mod, nw, w_bf, qnw, knw, cos_t, sin_t, *, tile, tk):
    b, s, d = x.shape
    if tile >= tk:
        assert tile % tk == 0
        vt_spec = pl.BlockSpec((1, N_KV_HEADS, tile // tk, HEAD_DIM, tk),
                               lambda bi, i: (bi, 0, i, 0, 0))
    else:
        assert tk % tile == 0
        per = tk // tile
        vt_spec = pl.BlockSpec((1, N_KV_HEADS, 1, HEAD_DIM, tile),
                               lambda bi, i: (bi, 0, i // per, 0, i % per))
    in_cols = w_bf.shape[1]
    attn_w = N_HEADS * HEAD_DIM
    kv_w = N_KV_HEADS * HEAD_DIM
    pool_w = in_cols - attn_w - 2 * kv_w - 2 * d
    tok = lambda width: pl.BlockSpec((1, tile, width), lambda bi, i: (bi, i, 0))
    full = lambda a: pl.BlockSpec(a.shape, lambda bi, i: (0,) * a.ndim)
    return pl.pallas_call(
        functools.partial(_in_proj_kernel, d_model=d, tk=tk),
        grid=(b, s // tile),
        in_specs=[
            tok(d),
            pl.BlockSpec((1, N_ADA, d), lambda bi, i: (bi, 0, 0)),
            full(nw), full(w_bf), full(qnw), full(knw),
            pl.BlockSpec((tile, HEAD_DIM), lambda bi, i: (i, 0)),
            pl.BlockSpec((tile, HEAD_DIM), lambda bi, i: (i, 0)),
        ],
        out_specs=[tok(attn_w), tok(kv_w), vt_spec, tok(pool_w), tok(d), tok(d)],
        out_shape=[
            jax.ShapeDtypeStruct((b, s, attn_w), BF16),
            jax.ShapeDtypeStruct((b, s, kv_w), BF16),
            jax.ShapeDtypeStruct((b, N_KV_HEADS, s // tk, HEAD_DIM, tk), BF16),
            jax.ShapeDtypeStruct((b, s, pool_w), F32),
            jax.ShapeDtypeStruct((b, s, d), BF16),
            jax.ShapeDtypeStruct((b, s, d), BF16),
        ],
        compiler_params=_cparams("parallel", "parallel"),
        name="in_proj",
    )(x, mod, nw, w_bf, qnw, knw, cos_t, sin_t)


def _flash_kernel(q_ref, k_ref, vt_ref, o_ref, m_scr, l_scr, acc_scr, *, tq, tk, seq):
    q4 = jnp.concatenate(
        [q_ref[0, :, g * HEAD_DIM:(g + 1) * HEAD_DIM] for g in range(GROUP)], axis=0)

    def write_out(acc, l):
        out = (acc / l).T
        for g in range(GROUP):
            o_ref[0, :, g * HEAD_DIM:(g + 1) * HEAD_DIM] = out[g * tq:(g + 1) * tq].astype(BF16)

    if tk == seq:
        s_t = _nt_dot(k_ref[0], q4)
        p_t = jnp.exp2(s_t - jnp.max(s_t, axis=0, keepdims=True))
        l = jnp.sum(p_t, axis=0, keepdims=True)
        acc = jnp.dot(vt_ref[0, 0, 0], p_t.astype(BF16), preferred_element_type=F32)
        write_out(acc, l)
        return

    m_scr[...] = jnp.full(m_scr.shape, -jnp.inf, F32)
    l_scr[...] = jnp.zeros(l_scr.shape, F32)
    acc_scr[...] = jnp.zeros(acc_scr.shape, F32)

    def body(j, carry):
        off = pl.multiple_of(j * tk, tk)
        kc = k_ref[0, pl.ds(off, tk), :]
        s_t = _nt_dot(kc, q4)
        m_prev = m_scr[...]
        m_new = jnp.maximum(m_prev, jnp.max(s_t, axis=0, keepdims=True))
        alpha = jnp.exp2(m_prev - m_new)
        p_t = jnp.exp2(s_t - m_new)
        l_scr[...] = alpha * l_scr[...] + jnp.sum(p_t, axis=0, keepdims=True)
        acc_scr[...] = alpha * acc_scr[...] + jnp.dot(
            vt_ref[0, 0, j], p_t.astype(BF16), preferred_element_type=F32)
        m_scr[...] = m_new
        return carry

    lax.fori_loop(0, seq // tk, body, 0)
    write_out(acc_scr[...], l_scr[...])


def _flash_attention(q, k, vt, *, tq, tk):
    b, s, _ = q.shape
    gw = GROUP * HEAD_DIM
    return pl.pallas_call(
        functools.partial(_flash_kernel, tq=tq, tk=tk, seq=s),
        grid=(b, N_KV_HEADS, s // tq),
        in_specs=[
            pl.BlockSpec((1, tq, gw), lambda bi, kv, i: (bi, i, kv)),
            pl.BlockSpec((1, s, HEAD_DIM), lambda bi, kv, i: (bi, 0, kv)),
            pl.BlockSpec((1, 1, s // tk, HEAD_DIM, tk), lambda bi, kv, i: (bi, kv, 0, 0, 0)),
        ],
        out_specs=pl.BlockSpec((1, tq, gw), lambda bi, kv, i: (bi, i, kv)),
        out_shape=jax.ShapeDtypeStruct(q.shape, BF16),
        scratch_shapes=[
            pltpu.VMEM((1, GROUP * tq), F32),
            pltpu.VMEM((1, GROUP * tq), F32),
            pltpu.VMEM((HEAD_DIM, GROUP * tq), F32),
        ],
        compiler_params=_cparams("parallel", "parallel", "parallel"),
        name="flash_gqa",
    )(q, k, vt)


def _mix_kernel(attn_ref, u_ref, up_ref, un_ref, ga_ref, gp_ref, x_ref, mod_ref,
                wao_ref, pmw_ref, ps_ref, wpo_ref, wout_ref, o_ref, ext_scr, *, tile, seq):
    i = pl.program_id(1)
    nblk = pl.num_programs(1)
    pool_w = u_ref.shape[-1]
    prev_ok = (i > 0).astype(F32)
    next_ok = (i < nblk - 1).astype(F32)
    ext_scr[0:POOL_HALO, :] = up_ref[0] * prev_ok
    ext_scr[POOL_HALO:POOL_HALO + tile, :] = u_ref[0]
    ext_scr[POOL_HALO + tile:POOL_HALO + tile + POOL_HALO, :] = un_ref[0] * next_ok

    t = i * tile + lax.broadcasted_iota(jnp.int32, (tile, POOL_GROUP_DIM), 0)
    mixed = []
    for gi, w in enumerate(POOL_WINDOWS):
        cols = slice(gi * POOL_GROUP_DIM, (gi + 1) * POOL_GROUP_DIM)
        half = w // 2
        win = ext_scr[POOL_HALO - half:POOL_HALO - half + tile, cols]
        for kk in range(1, w):
            a = POOL_HALO - half + kk
            win = win + ext_scr[a:a + tile, cols]
        cnt = (jnp.minimum(t + half, seq) - jnp.maximum(t - half, 0)).astype(F32)
        m = win / cnt - u_ref[0, :, cols]
        mg = jnp.dot(m.astype(BF16), pmw_ref[gi], preferred_element_type=F32)
        mixed.append((mg * ps_ref[:, cols]).astype(BF16))
    pool_in = jnp.concatenate(mixed, axis=1)
    pool_o = jnp.dot(pool_in, wpo_ref[...], preferred_element_type=F32)
    attn_o = jnp.dot(attn_ref[0], wao_ref[...], preferred_element_type=F32)
    merged = ga_ref[0].astype(F32) * attn_o + gp_ref[0].astype(F32) * pool_o
    y = jnp.dot(merged.astype(BF16), wout_ref[...], preferred_element_type=F32)
    g1 = mod_ref[0, 2:3, :]
    o_ref[0] = x_ref[0] + g1 * y
    del pool_w


def _mix_out(attn, u, sga, sgp, x, mod, wao, pmw, ps, wpo, wout, *, tile):
    b, s, d = x.shape
    pool_w = u.shape[-1]
    hb = tile // POOL_HALO
    n_halo = s // POOL_HALO
    tok = lambda width: pl.BlockSpec((1, tile, width), lambda bi, i: (bi, i, 0))
    full = lambda a: pl.BlockSpec(a.shape, lambda bi, i: (0,) * a.ndim)
    return pl.pallas_call(
        functools.partial(_mix_kernel, tile=tile, seq=s),
        grid=(b, s // tile),
        in_specs=[
            tok(d), tok(pool_w),
            pl.BlockSpec((1, POOL_HALO, pool_w),
                         lambda bi, i: (bi, jnp.maximum(i * hb - 1, 0), 0)),
            pl.BlockSpec((1, POOL_HALO, pool_w),
                         lambda bi, i: (bi, jnp.minimum((i + 1) * hb, n_halo - 1), 0)),
            tok(d), tok(d), tok(d),
            pl.BlockSpec((1, N_ADA, d), lambda bi, i: (bi, 0, 0)),
            full(wao), full(pmw), full(ps), full(wpo), full(wout),
        ],
        out_specs=tok(d),
        out_shape=jax.ShapeDtypeStruct(x.shape, F32),
        scratch_shapes=[pltpu.VMEM((tile + 2 * POOL_HALO, pool_w), F32)],
        compiler_params=_cparams("parallel", "parallel"),
        name="mix_out",
    )(attn, u, u, u, sga, sgp, x, mod, wao, pmw, ps, wpo, wout)


_CAND_B_COUNTS = (16, 8, 8, 8, 8, 8, 8, 8)


def _top16_rows(s, row_iota, exact):
    vals = []
    rank = jnp.full(s.shape, NOT_SELECTED_RANK, F32)
    work = s
    big = float(s.shape[0])
    for r in range(PEER_TOPK):
        m = jnp.max(work, axis=0, keepdims=True)
        sel = work == m
        if exact:
            pos = jnp.min(jnp.where(sel, row_iota, big), axis=0, keepdims=True)
            sel = row_iota == pos
        vals.append(m)
        rank = jnp.where(sel, float(r), rank)
        work = jnp.where(sel, -jnp.inf, work)
    selected = jnp.where(rank < NOT_SELECTED_RANK, 1.0, 0.0)
    count = jnp.sum(selected, axis=0, keepdims=True)
    return jnp.concatenate(vals, axis=0), rank, selected, count


def _bf16_pair_bits(v):
    bits = pltpu.bitcast(v.astype(BF16).astype(F32), jnp.uint32)
    return bits | (bits >> 16)


def _route_kernel(x_ref, mod_ref, nw_ref, wq_ref, sk_ref,
                  xf_ref, e0_ref, n0_ref, e1_ref, r1_ref, *, tile):
    x = x_ref[0]
    xn = x * lax.rsqrt(jnp.mean(x * x, axis=-1, keepdims=True) + EPS) * nw_ref[...]
    shift = mod_ref[0, 3:4, :]
    scale = mod_ref[0, 4:5, :]
    xf = (xn * (1.0 + scale) + shift).astype(BF16)
    xf_ref[0] = xf
    q_t = _nt_dot(wq_ref[...], xf)

    key_iota = lax.broadcasted_iota(jnp.int32, (N_KEYS, tile), 0).astype(F32)
    n_cand = sum(_CAND_B_COUNTS) + (PEER_TOPK - len(_CAND_B_COUNTS))
    cand_iota = lax.broadcasted_iota(jnp.int32, (n_cand, tile), 0).astype(F32)

    def route_heads(exact):
        miscount = jnp.zeros((1, tile), F32)
        for h in range(N_PEER_HEADS):
            scores = []
            for p in range(2):
                r0 = (h * 2 + p) * SUBKEY_DIM
                qs = q_t[r0:r0 + SUBKEY_DIM, :].astype(BF16)
                scores.append(jnp.dot(sk_ref[h, p], qs, preferred_element_type=F32))
            v0, rank0, _, cnt0 = _top16_rows(scores[0], key_iota, exact)
            v1, rank1, _, cnt1 = _top16_rows(scores[1], key_iota, exact)

            blocks = []
            for a, nb in enumerate(_CAND_B_COUNTS):
                blocks.append(v0[a:a + 1, :] + v1[0:nb, :])
            blocks.append(v0[len(_CAND_B_COUNTS):, :] + v1[0:1, :])
            cand = jnp.concatenate(blocks, axis=0)
            _, _, selected, cnt2 = _top16_rows(cand, cand_iota, exact)
            z = jnp.sum(selected * jnp.exp(cand - cand[0:1, :]), axis=0, keepdims=True)
            for cnt in (cnt0, cnt1, cnt2):
                miscount = jnp.maximum(miscount, jnp.abs(cnt - float(PEER_TOPK)))

            n0 = jnp.zeros((N_KEYS, tile), F32)
            row = 0
            for a, nb in enumerate(_CAND_B_COUNTS):
                n_a = jnp.sum(selected[row:row + nb, :], axis=0, keepdims=True)
                n0 = jnp.where(rank0 == float(a), n_a, n0)
                row += nb
            for a in range(len(_CAND_B_COUNTS), PEER_TOPK):
                n0 = jnp.where(rank0 == float(a), selected[row:row + 1, :], n0)
                row += 1

            e0_ref[h] = _bf16_pair_bits(jnp.exp(scores[0] - v0[0:1, :]) / z)
            n0_ref[h] = _bf16_pair_bits(n0)
            e1_ref[h] = jnp.exp(scores[1] - v1[0:1, :]).astype(BF16)
            r1_ref[h] = rank1.astype(BF16)
        return miscount

    miscount = route_heads(exact=False)

    @pl.when(jnp.max(miscount) > 0.0)
    def _():
        route_heads(exact=True)


def _peer_route(x1, mod, nw, wq_t, sk_bf, *, tile):
    b, s, d = x1.shape
    n_tok = b * s
    nblk = s // tile
    dense = lambda: pl.BlockSpec((N_PEER_HEADS, N_KEYS, tile), lambda bi, i: (0, 0, bi * nblk + i))
    full = lambda a: pl.BlockSpec(a.shape, lambda bi, i: (0,) * a.ndim)
    pair_shape = jax.ShapeDtypeStruct((N_PEER_HEADS, N_KEYS, n_tok), jnp.uint32)
    bf_shape = jax.ShapeDtypeStruct((N_PEER_HEADS, N_KEYS, n_tok), BF16)
    return pl.pallas_call(
        functools.partial(_route_kernel, tile=tile),
        grid=(b, nblk),
        in_specs=[
            pl.BlockSpec((1, tile, d), lambda bi, i: (bi, i, 0)),
            pl.BlockSpec((1, N_ADA, d), lambda bi, i: (bi, 0, 0)),
            full(nw), full(wq_t), full(sk_bf),
        ],
        out_specs=[pl.BlockSpec((1, tile, d), lambda bi, i: (bi, i, 0)),
                   dense(), dense(), dense(), dense()],
        out_shape=[jax.ShapeDtypeStruct((b, s, d), BF16),
                   pair_shape, pair_shape, bf_shape, bf_shape],
        compiler_params=_cparams("parallel", "parallel"),
        name="peer_route",
    )(x1, mod, nw, wq_t, sk_bf)


def _expert_kernel(xf_ref, e0_ref, n0_ref, e1_ref, r1_ref, u_ref, vt_ref, x_ref, mod_ref, fnw_ref,
                   o_ref, xft_scr, act_a, act_b, y_scr, *, tile, keys_per_chunk, n_chunks,
                   final_norm):
    ec = pl.program_id(2)
    chunk_id = jnp.minimum(ec, n_chunks - 1)

    @pl.when(ec == 0)
    def _():
        y_scr[...] = jnp.zeros(y_scr.shape, F32)
        act_b[...] = jnp.zeros(act_b.shape, BF16)
        xft_scr[...] = xf_ref[0].astype(F32).T.astype(BF16)

    def row_bf16(ref, h, key):
        word = ref[h, pl.ds(key, 1), :]
        return pltpu.bitcast(jnp.broadcast_to(word, (N_KEYS // 2, tile)), BF16)

    d_model = y_scr.shape[0]
    n_pieces = keys_per_chunk // KEYS_PER_PIECE
    out_rows = d_model // n_pieces

    def stage(act_write, act_read):
        for p in range(n_pieces):
            rows = slice(p * out_rows, (p + 1) * out_rows)
            y_scr[rows, :] += jnp.dot(vt_ref[rows, :], act_read[...], preferred_element_type=F32)
            e_lo = p * KEYS_PER_PIECE * N_KEYS
            hid_p = jnp.dot(u_ref[e_lo:e_lo + KEYS_PER_PIECE * N_KEYS, :], xft_scr[...],
                            preferred_element_type=F32)
            for kk in range(KEYS_PER_PIECE):
                ii = p * KEYS_PER_PIECE + kk
                key = chunk_id * keys_per_chunk + ii
                gate = jnp.zeros((N_KEYS, tile), BF16)
                for h in range(N_PEER_HEADS):
                    e0 = row_bf16(e0_ref, h, key)
                    n0 = row_bf16(n0_ref, h, key)
                    gate = gate + jnp.where(r1_ref[h] < n0, e0 * e1_ref[h], jnp.zeros((), BF16))
                hid = hid_p[kk * N_KEYS:(kk + 1) * N_KEYS, :]
                act = 0.5 * hid * (1.0 + lax.erf(hid * (1.0 / math.sqrt(2.0))))
                act_write[ii * N_KEYS:(ii + 1) * N_KEYS, :] = act.astype(BF16) * gate

    @pl.when(ec % 2 == 0)
    def _():
        stage(act_a, act_b)

    @pl.when(ec % 2 == 1)
    def _():
        stage(act_b, act_a)

    @pl.when(ec == n_chunks)
    def _():
        g2 = mod_ref[0, 5:6, :]
        out = x_ref[0] + g2 * y_scr[...].T
        if final_norm:
            out = out * lax.rsqrt(jnp.mean(out * out, axis=-1, keepdims=True) + EPS) * fnw_ref[...]
        o_ref[0] = out


def _peer_experts(xf, e0, n0, e1, r1, u_bf, vt_bf, x1, mod, fnw, *, tile, keys_per_chunk,
                  final_norm):
    b, s, d = x1.shape
    nblk = s // tile
    n_exp = u_bf.shape[0]
    chunk = keys_per_chunk * N_KEYS
    dense = lambda: pl.BlockSpec((N_PEER_HEADS, N_KEYS, tile),
                                 lambda bi, i, e: (0, 0, bi * nblk + i))
    tok = lambda: pl.BlockSpec((1, tile, d), lambda bi, i, e: (bi, i, 0))
    n_chunks = n_exp // chunk
    return pl.pallas_call(
        functools.partial(_expert_kernel, tile=tile, keys_per_chunk=keys_per_chunk,
                          n_chunks=n_chunks, final_norm=final_norm),
        grid=(b, nblk, n_chunks + 1),
        in_specs=[
            tok(), dense(), dense(), dense(), dense(),
            pl.BlockSpec((chunk, d), lambda bi, i, e: (jnp.minimum(e, n_chunks - 1), 0)),
            pl.BlockSpec((d, chunk), lambda bi, i, e: (0, jnp.maximum(e - 1, 0))),
            tok(),
            pl.BlockSpec((1, N_ADA, d), lambda bi, i, e: (bi, 0, 0)),
            pl.BlockSpec(fnw.shape, lambda bi, i, e: (0, 0)),
        ],
        out_specs=tok(),
        out_shape=jax.ShapeDtypeStruct(x1.shape, F32),
        scratch_shapes=[
            pltpu.VMEM((d, tile), BF16),
            pltpu.VMEM((chunk, tile), BF16),
            pltpu.VMEM((chunk, tile), BF16),
            pltpu.VMEM((d, tile), F32),
        ],
        compiler_params=_cparams("parallel", "parallel", "arbitrary"),
        name="peer_experts",
    )(xf, e0, n0, e1, r1, u_bf, vt_bf, x1, mod, fnw)


def _rope_tables(seq):
    t = jnp.arange(seq)
    row = (t // GRID_W).astype(F32)
    col = (t % GRID_W).astype(F32)
    inv = 1.0 / (ROPE_THETA ** (jnp.arange(0, ROPE_HALF, 2, dtype=F32) / ROPE_HALF))
    ang_r = row[:, None] * inv
    ang_c = col[:, None] * inv
    cos_t = jnp.concatenate([jnp.cos(ang_r), jnp.cos(ang_r), jnp.cos(ang_c), jnp.cos(ang_c)], axis=1)
    sin_t = jnp.concatenate([-jnp.sin(ang_r), jnp.sin(ang_r), -jnp.sin(ang_c), jnp.sin(ang_c)], axis=1)
    return cos_t, sin_t


def _pick_tile(seq, want):
    tile = min(want, seq)
    assert seq % tile == 0
    return tile


def kernel(x, c, ada_w, ada_b, norm_mix_w, w_in, q_norm_w, k_norm_w, w_attn_o, pool_mix_w, pool_scale, w_pool_o, w_out, norm_ffn_w, w_query, sub_keys, expert_u, expert_v, final_norm_w):
    b, s, d = x.shape
    depth = ada_w.shape[0]
    cos_t, sin_t = _rope_tables(s)

    rows = 8
    c_pad = jnp.zeros((rows, d), F32).at[:b].set(c)
    mod_all = _ada_mod(c_pad, ada_w, ada_b)[:, :b].reshape(depth, b, N_ADA, d)

    t_proj = _pick_tile(s, 512)
    t_q = _pick_tile(s, 256)
    single_block_bytes = s * GROUP * t_q * 6
    t_k = s if single_block_bytes <= ATTN_SCORE_VMEM_BYTES else _pick_tile(s, 512)
    t_mix = _pick_tile(s, 512)
    t_route = _pick_tile(s, 256)
    t_exp = _pick_tile(s, 512)

    for l in range(depth):
        mod = mod_all[l]
        q, k, vt, u, sga, sgp = _in_proj(
            x, mod, norm_mix_w[l][None], w_in[l].astype(BF16), q_norm_w[l][None],
            k_norm_w[l][None], cos_t, sin_t, tile=t_proj, tk=t_k)
        attn = _flash_attention(q, k, vt, tq=t_q, tk=t_k)
        x1 = _mix_out(attn, u, sga, sgp, x, mod, w_attn_o[l].astype(BF16),
                      pool_mix_w[l].astype(BF16), pool_scale[l][None],
                      w_pool_o[l].astype(BF16), w_out[l].astype(BF16), tile=t_mix)
        xf, e0, n0, e1, r1 = _peer_route(
            x1, mod, norm_ffn_w[l][None], w_query[l].T.astype(BF16),
            sub_keys[l].astype(BF16), tile=t_route)
        x = _peer_experts(
            xf, e0, n0, e1, r1, expert_u[l].astype(BF16), expert_v[l].T.astype(BF16), x1, mod,
            final_norm_w[None], tile=t_exp, keys_per_chunk=8, final_norm=(l == depth - 1))
    return x
```

```python
import functools
import math

import jax
import jax.numpy as jnp
import numpy as np
from jax import lax
from jax.experimental import pallas as pl
from jax.experimental.pallas import tpu as pltpu

F32 = jnp.float32
BF16 = jnp.bfloat16

EPS = 1e-6
GRID_W = 64
HEAD_DIM = 128
N_HEADS = 8
N_KV_HEADS = 2
GROUP = N_HEADS // N_KV_HEADS
ROPE_THETA = 10000.0
ROPE_HALF = HEAD_DIM // 2
POOL_WINDOWS = (2, 4, 8, 16)
POOL_GROUP_DIM = 128
POOL_HALO = 8
N_PEER_HEADS = 8
SUBKEY_DIM = 256
N_KEYS = 128
PEER_TOPK = 16
N_ADA = 6
NOT_SELECTED_RANK = 99.0

VMEM_LIMIT_BYTES = 56 * 1024 * 1024
ATTN_SCORE_VMEM_BYTES = 48 * 1024 * 1024


def _cparams(*sem):
    return pltpu.CompilerParams(dimension_semantics=sem, vmem_limit_bytes=VMEM_LIMIT_BYTES)


def _nt_dot(a, b):
    return lax.dot_general(a, b, (((1,), (1,)), ((), ())), preferred_element_type=F32)


def _ada_kernel(c_ref, w_ref, b_ref, o_ref):
    c = c_ref[...]
    c_act = c * jax.nn.sigmoid(c)
    o_ref[0] = jnp.dot(c_act, w_ref[0], preferred_element_type=F32,
                       precision=lax.Precision.HIGHEST) + b_ref[0]


def _ada_mod(c_pad, ada_w, ada_b):
    depth, d, n = ada_w.shape
    rows = c_pad.shape[0]
    tn = 1536
    return pl.pallas_call(
        _ada_kernel,
        grid=(depth, n // tn),
        in_specs=[
            pl.BlockSpec((rows, d), lambda l, j: (0, 0)),
            pl.BlockSpec((1, d, tn), lambda l, j: (l, 0, j)),
            pl.BlockSpec((1, 1, tn), lambda l, j: (l, 0, j)),
        ],
        out_specs=pl.BlockSpec((1, rows, tn), lambda l, j: (l, 0, j)),
        out_shape=jax.ShapeDtypeStruct((depth, rows, n), F32),
        compiler_params=_cparams("parallel", "parallel"),
        name="ada_mod",
    )(c_pad, ada_w, ada_b.reshape(depth, 1, n))


def _rms_rope(h, nw, cos, sin, low_half):
    hn = h * lax.rsqrt(jnp.mean(h * h, axis=-1, keepdims=True) + EPS) * nw
    swapped = jnp.where(low_half, pltpu.roll(hn, 96, 1), pltpu.roll(hn, 32, 1))
    return hn * cos + swapped * sin


def _in_proj_kernel(x_ref, mod_ref, nw_ref, w_ref, qnw_ref, knw_ref, cos_ref, sin_ref,
                    q_ref, k_ref, vt_ref, u_ref, ga_ref, gp_ref, *, d_model, tk):
    x = x_ref[0]
    xn = x * lax.rsqrt(jnp.mean(x * x, axis=-1, keepdims=True) + EPS) * nw_ref[...]
    shift = mod_ref[0, 0:1, :]
    scale = mod_ref[0, 1:2, :]
    xm = (xn * (1.0 + scale) + shift).astype(BF16)
    proj = jnp.dot(xm, w_ref[...], preferred_element_type=F32)

    cos = cos_ref[...]
    sin = sin_ref[...]
    lane = lax.broadcasted_iota(jnp.int32, cos.shape, 1)
    low_half = (lane % ROPE_HALF) < (ROPE_HALF // 2)
    sm_scale = math.log2(math.e) / math.sqrt(HEAD_DIM)
    attn_w = N_HEADS * HEAD_DIM
    kv_w = N_KV_HEADS * HEAD_DIM
    for h in range(N_HEADS):
        qh = _rms_rope(proj[:, h * HEAD_DIM:(h + 1) * HEAD_DIM], qnw_ref[...], cos, sin, low_half)
        q_ref[0, :, h * HEAD_DIM:(h + 1) * HEAD_DIM] = (qh * sm_scale).astype(BF16)
    for h in range(N_KV_HEADS):
        c0 = attn_w + h * HEAD_DIM
        kh = _rms_rope(proj[:, c0:c0 + HEAD_DIM], knw_ref[...], cos, sin, low_half)
        k_ref[0, :, h * HEAD_DIM:(h + 1) * HEAD_DIM] = kh.astype(BF16)
    c0 = attn_w + kv_w
    v_t = proj[:, c0:c0 + kv_w].T.astype(BF16)
    vw = vt_ref.shape[-1]
    for h in range(N_KV_HEADS):
        for cj in range(v_t.shape[1] // vw):
            vt_ref[0, h, cj] = v_t[h * HEAD_DIM:(h + 1) * HEAD_DIM, cj * vw:(cj + 1) * vw]
    del tk
    c0 += kv_w
    pool_w = u_ref.shape[-1]
    u_ref[0] = proj[:, c0:c0 + pool_w]
    c0 += pool_w
    ga_ref[0] = jax.nn.sigmoid(proj[:, c0:c0 + d_model]).astype(BF16)
    c0 += d_model
    gp_ref[0] = jax.nn.sigmoid(proj[:, c0:c0 + d_model]).astype(BF16)


def _in_proj(x, mod, nw, w_bf, qnw, knw, cos_t, sin_t, *, tile, tk):
    b, s, d = x.shape
    if tile >= tk:
        assert tile % tk == 0
        vt_spec = pl.BlockSpec((1, N_KV_HEADS, tile // tk, HEAD_DIM, tk),
                               lambda bi, i: (bi, 0, i, 0, 0))
    else:
        assert tk % tile == 0
        per = tk // tile
        vt_spec = pl.BlockSpec((1, N_KV_HEADS, 1, HEAD_DIM, tile),
                               lambda bi, i: (bi, 0, i // per, 0, i % per))
    in_cols = w_bf.shape[1]
    attn_w = N_HEADS * HEAD_DIM
    kv_w = N_KV_HEADS * HEAD_DIM
    pool_w = in_cols - attn_w - 2 * kv_w - 2 * d
    tok = lambda width: pl.BlockSpec((1, tile, width), lambda bi, i: (bi, i, 0))
    full = lambda a: pl.BlockSpec(a.shape, lambda bi, i: (0,) * a.ndim)
    return pl.pallas_call(
        functools.partial(_in_proj_kernel, d_model=d, tk=tk),
        grid=(b, s // tile),
        in_specs=[
            tok(d),
            pl.BlockSpec((1, N_ADA, d), lambda bi, i: (bi, 0, 0)),
            full(nw), full(w_bf), full(qnw), full(knw),
            pl.BlockSpec((tile, HEAD_DIM), lambda bi, i: (i, 0)),
            pl.BlockSpec((tile, HEAD_DIM), lambda bi, i: (i, 0)),
        ],
        out_specs=[tok(attn_w), tok(kv_w), vt_spec, tok(pool_w), tok(d), tok(d)],
        out_shape=[
            jax.ShapeDtypeStruct((b, s, attn_w), BF16),
            jax.ShapeDtypeStruct((b, s, kv_w), BF16),
            jax.ShapeDtypeStruct((b, N_KV_HEADS, s // tk, HEAD_DIM, tk), BF16),
            jax.ShapeDtypeStruct((b, s, pool_w), F32),
            jax.ShapeDtypeStruct((b, s, d), BF16),
            jax.ShapeDtypeStruct((b, s, d), BF16),
        ],
        compiler_params=_cparams("parallel", "parallel"),
        name="in_proj",
    )(x, mod, nw, w_bf, qnw, knw, cos_t, sin_t)


def _flash_kernel(q_ref, k_ref, vt_ref, o_ref, m_scr, l_scr, acc_scr, *, tq, tk, seq):
    q4 = jnp.concatenate(
        [q_ref[0, :, g * HEAD_DIM:(g + 1) * HEAD_DIM] for g in range(GROUP)], axis=0)

    def write_out(acc, l):
        out = (acc / l).T
        for g in range(GROUP):
            o_ref[0, :, g * HEAD_DIM:(g + 1) * HEAD_DIM] = out[g * tq:(g + 1) * tq].astype(BF16)

    if tk == seq:
        s_t = _nt_dot(k_ref[0], q4)
        p_t = jnp.exp2(s_t - jnp.max(s_t, axis=0, keepdims=True))
        l = jnp.sum(p_t, axis=0, keepdims=True)
        acc = jnp.dot(vt_ref[0, 0, 0], p_t.astype(BF16), preferred_element_type=F32)
        write_out(acc, l)
        return

    m_scr[...] = jnp.full(m_scr.shape, -jnp.inf, F32)
    l_scr[...] = jnp.zeros(l_scr.shape, F32)
    acc_scr[...] = jnp.zeros(acc_scr.shape, F32)

    def body(j, carry):
        off = pl.multiple_of(j * tk, tk)
        kc = k_ref[0, pl.ds(off, tk), :]
        s_t = _nt_dot(kc, q4)
        m_prev = m_scr[...]
        m_new = jnp.maximum(m_prev, jnp.max(s_t, axis=0, keepdims=True))
        alpha = jnp.exp2(m_prev - m_new)
        p_t = jnp.exp2(s_t - m_new)
        l_scr[...] = alpha * l_scr[...] + jnp.sum(p_t, axis=0, keepdims=True)
        acc_scr[...] = alpha * acc_scr[...] + jnp.dot(
            vt_ref[0, 0, j], p_t.astype(BF16), preferred_element_type=F32)
        m_scr[...] = m_new
        return carry

    lax.fori_loop(0, seq // tk, body, 0)
    write_out(acc_scr[...], l_scr[...])


def _flash_attention(q, k, vt, *, tq, tk):
    b, s, _ = q.shape
    gw = GROUP * HEAD_DIM
    return pl.pallas_call(
        functools.partial(_flash_kernel, tq=tq, tk=tk, seq=s),
        grid=(b, N_KV_HEADS, s // tq),
        in_specs=[
            pl.BlockSpec((1, tq, gw), lambda bi, kv, i: (bi, i, kv)),
            pl.BlockSpec((1, s, HEAD_DIM), lambda bi, kv, i: (bi, 0, kv)),
            pl.BlockSpec((1, 1, s // tk, HEAD_DIM, tk), lambda bi, kv, i: (bi, kv, 0, 0, 0)),
        ],
        out_specs=pl.BlockSpec((1, tq, gw), lambda bi, kv, i: (bi, i, kv)),
        out_shape=jax.ShapeDtypeStruct(q.shape, BF16),
        scratch_shapes=[
            pltpu.VMEM((1, GROUP * tq), F32),
            pltpu.VMEM((1, GROUP * tq), F32),
            pltpu.VMEM((HEAD_DIM, GROUP * tq), F32),
        ],
        compiler_params=_cparams("parallel", "parallel", "parallel"),
        name="flash_gqa",
    )(q, k, vt)


def _mix_kernel(attn_ref, u_ref, up_ref, un_ref, ga_ref, gp_ref, x_ref, mod_ref,
                wao_ref, pmw_ref, ps_ref, wpo_ref, wout_ref, o_ref, ext_scr, *, tile, seq):
    i = pl.program_id(1)
    nblk = pl.num_programs(1)
    pool_w = u_ref.shape[-1]
    prev_ok = (i > 0).astype(F32)
    next_ok = (i < nblk - 1).astype(F32)
    ext_scr[0:POOL_HALO, :] = up_ref[0] * prev_ok
    ext_scr[POOL_HALO:POOL_HALO + tile, :] = u_ref[0]
    ext_scr[POOL_HALO + tile:POOL_HALO + tile + POOL_HALO, :] = un_ref[0] * next_ok

    t = i * tile + lax.broadcasted_iota(jnp.int32, (tile, POOL_GROUP_DIM), 0)
    mixed = []
    for gi, w in enumerate(POOL_WINDOWS):
        cols = slice(gi * POOL_GROUP_DIM, (gi + 1) * POOL_GROUP_DIM)
        half = w // 2
        win = ext_scr[POOL_HALO - half:POOL_HALO - half + tile, cols]
        for kk in range(1, w):
            a = POOL_HALO - half + kk
            win = win + ext_scr[a:a + tile, cols]
        cnt = (jnp.minimum(t + half, seq) - jnp.maximum(t - half, 0)).astype(F32)
        m = win / cnt - u_ref[0, :, cols]
        mg = jnp.dot(m.astype(BF16), pmw_ref[gi], preferred_element_type=F32)
        mixed.append((mg * ps_ref[:, cols]).astype(BF16))
    pool_in = jnp.concatenate(mixed, axis=1)
    pool_o = jnp.dot(pool_in, wpo_ref[...], preferred_element_type=F32)
    attn_o = jnp.dot(attn_ref[0], wao_ref[...], preferred_element_type=F32)
    merged = ga_ref[0].astype(F32) * attn_o + gp_ref[0].astype(F32) * pool_o
    y = jnp.dot(merged.astype(BF16), wout_ref[...], preferred_element_type=F32)
    g1 = mod_ref[0, 2:3, :]
    o_ref[0] = x_ref[0] + g1 * y
    del pool_w


def _mix_out(attn, u, sga, sgp, x, mod, wao, pmw, ps, wpo, wout, *, tile):
    b, s, d = x.shape
    pool_w = u.shape[-1]
    hb = tile // POOL_HALO
    n_halo = s // POOL_HALO
    tok = lambda width: pl.BlockSpec((1, tile, width), lambda bi, i: (bi, i, 0))
    full = lambda a: pl.BlockSpec(a.shape, lambda bi, i: (0,) * a.ndim)
    return pl.pallas_call(
        functools.partial(_mix_kernel, tile=tile, seq=s),
        grid=(b, s // tile),
        in_specs=[
            tok(d), tok(pool_w),
            pl.BlockSpec((1, POOL_HALO, pool_w),
                         lambda bi, i: (bi, jnp.maximum(i * hb - 1, 0), 0)),
            pl.BlockSpec((1, POOL_HALO, pool_w),
                         lambda bi, i: (bi, jnp.minimum((i + 1) * hb, n_halo - 1), 0)),
            tok(d), tok(d), tok(d),
            pl.BlockSpec((1, N_ADA, d), lambda bi, i: (bi, 0, 0)),
            full(wao), full(pmw), full(ps), full(wpo), full(wout),
        ],
        out_specs=tok(d),
        out_shape=jax.ShapeDtypeStruct(x.shape, F32),
        scratch_shapes=[pltpu.VMEM((tile + 2 * POOL_HALO, pool_w), F32)],
        compiler_params=_cparams("parallel", "parallel"),
        name="mix_out",
    )(attn, u, u, u, sga, sgp, x, mod, wao, pmw, ps, wpo, wout)


_CAND_B_COUNTS = (16, 8, 8, 8, 8, 8, 8, 8)


def _top16_rows(s, row_iota, exact):
    vals = []
    rank = jnp.full(s.shape, NOT_SELECTED_RANK, F32)
    work = s
    big = float(s.shape[0])
    for r in range(PEER_TOPK):
        m = jnp.max(work, axis=0, keepdims=True)
        sel = work == m
        if exact:
            pos = jnp.min(jnp.where(sel, row_iota, big), axis=0, keepdims=True)
            sel = row_iota == pos
        vals.append(m)
        rank = jnp.where(sel, float(r), rank)
        work = jnp.where(sel, -jnp.inf, work)
    selected = jnp.where(rank < NOT_SELECTED_RANK, 1.0, 0.0)
    count = jnp.sum(selected, axis=0, keepdims=True)
    return jnp.concatenate(vals, axis=0), rank, selected, count


def _oddeven_mergesort_pairs(n):
    pairs = []

    def merge(lo, hi, r):
        step = r * 2
        if step < hi - lo:
            merge(lo, hi, step)
            merge(lo + r, hi, step)
            pairs.extend((i, i + r) for i in range(lo + r, hi - r, step))
        else:
            pairs.append((lo, lo + r))

    def sort(lo, hi):
        if hi - lo >= 1:
            mid = lo + (hi - lo) // 2
            sort(lo, mid)
            sort(mid + 1, hi)
            merge(lo, hi, 1)

    sort(0, n - 1)
    return pairs


SUBLANES = 8
_SORT16_PAIRS = _oddeven_mergesort_pairs(PEER_TOPK)


def _sorted_top16(s):
    assert s.shape[0] == PEER_TOPK * SUBLANES
    slabs = [s[k * SUBLANES:(k + 1) * SUBLANES, :] for k in range(PEER_TOPK)]

    def exchange(i, j):
        slabs[i], slabs[j] = jnp.maximum(slabs[i], slabs[j]), jnp.minimum(slabs[i], slabs[j])

    for i, j in _SORT16_PAIRS:
        exchange(i, j)
    shift = SUBLANES // 2
    while shift >= 1:
        rolled = [pltpu.roll(x, shift, 0) for x in slabs]
        slabs = [jnp.maximum(slabs[k], rolled[PEER_TOPK - 1 - k]) for k in range(PEER_TOPK)]
        stride = PEER_TOPK // 2
        while stride >= 1:
            for k in range(PEER_TOPK):
                if k & stride == 0:
                    exchange(k, k + stride)
            stride //= 2
        shift //= 2
    return slabs


def _count_by_slab(s, fn):
    out = []
    for r in range(s.shape[0] // SUBLANES):
        x = s[r * SUBLANES:(r + 1) * SUBLANES, :]
        acc = jnp.zeros(x.shape, F32)
        for k in range(PEER_TOPK):
            acc = fn(acc, x, k)
        out.append(acc)
    return jnp.concatenate(out, axis=0)


def _bf16_pair_bits(v):
    bits = pltpu.bitcast(v.astype(BF16).astype(F32), jnp.uint32)
    return bits | (bits >> 16)


def _route_kernel(x_ref, mod_ref, nw_ref, wq_ref, sk_ref,
                  xf_ref, e0_ref, n0_ref, e1_ref, r1_ref, *, tile):
    x = x_ref[0]
    xn = x * lax.rsqrt(jnp.mean(x * x, axis=-1, keepdims=True) + EPS) * nw_ref[...]
    shift = mod_ref[0, 3:4, :]
    scale = mod_ref[0, 4:5, :]
    xf = (xn * (1.0 + scale) + shift).astype(BF16)
    xf_ref[0] = xf
    q_t = _nt_dot(wq_ref[...], xf)

    key_iota = lax.broadcasted_iota(jnp.int32, (N_KEYS, tile), 0).astype(F32)
    n_cand = sum(_CAND_B_COUNTS) + (PEER_TOPK - len(_CAND_B_COUNTS))
    cand_iota = lax.broadcasted_iota(jnp.int32, (n_cand, tile), 0).astype(F32)

    def head_scores(h):
        scores = []
        for p in range(2):
            r0 = (h * 2 + p) * SUBKEY_DIM
            qs = q_t[r0:r0 + SUBKEY_DIM, :].astype(BF16)
            scores.append(jnp.dot(sk_ref[h, p], qs, preferred_element_type=F32))
        return scores

    def candidate_sums(v0, v1):
        blocks = []
        for a, nb in enumerate(_CAND_B_COUNTS):
            blocks.append(v0[a:a + 1, :] + v1[0:nb, :])
        blocks.append(v0[len(_CAND_B_COUNTS):, :] + v1[0:1, :])
        return jnp.concatenate(blocks, axis=0)

    def per_rank_counts(selected):
        counts = []
        row = 0
        for nb in _CAND_B_COUNTS:
            counts.append(jnp.sum(selected[row:row + nb, :], axis=0, keepdims=True))
            row += nb
        for _ in range(len(_CAND_B_COUNTS), PEER_TOPK):
            counts.append(selected[row:row + 1, :])
            row += 1
        return counts

    def write_head(h, scores, top0, top1, z, n0, rank1):
        e0_ref[h] = _bf16_pair_bits(jnp.exp(scores[0] - top0) / z)
        n0_ref[h] = _bf16_pair_bits(n0)
        e1_ref[h] = jnp.exp(scores[1] - top1).astype(BF16)
        r1_ref[h] = rank1.astype(BF16)

    def route_heads_exact():
        for h in range(N_PEER_HEADS):
            scores = head_scores(h)
            v0, rank0, _, _ = _top16_rows(scores[0], key_iota, True)
            v1, rank1, _, _ = _top16_rows(scores[1], key_iota, True)
            cand = candidate_sums(v0, v1)
            _, _, selected, _ = _top16_rows(cand, cand_iota, True)
            z = jnp.sum(selected * jnp.exp(cand - cand[0:1, :]), axis=0, keepdims=True)
            n0 = jnp.zeros((N_KEYS, tile), F32)
            for a, n_a in enumerate(per_rank_counts(selected)):
                n0 = jnp.where(rank0 == float(a), n_a, n0)
            write_head(h, scores, v0[0:1, :], v1[0:1, :], z, n0, rank1)

    def route_heads_fast():
        suspect = jnp.zeros((1, tile), F32)
        for h in range(N_PEER_HEADS):
            scores = head_scores(h)
            s0, s1 = scores
            v0 = _sorted_top16(s0)
            v1 = _sorted_top16(s1)
            rank1 = _count_by_slab(s1, lambda acc, x, k: acc + jnp.where(x < v1[k], 1.0, 0.0))
            v0_rows = jnp.concatenate([v[0:1, :] for v in v0], axis=0)
            v1_rows = jnp.concatenate([v[0:1, :] for v in v1], axis=0)
            cand = candidate_sums(v0_rows, v1_rows)
            _, _, selected, cnt2 = _top16_rows(cand, cand_iota, False)
            z = jnp.sum(selected * jnp.exp(cand - cand[0:1, :]), axis=0, keepdims=True)
            n_slabs = [jnp.broadcast_to(n_a, (SUBLANES, tile)) for n_a in per_rank_counts(selected)]
            n0 = _count_by_slab(s0, lambda acc, x, k: jnp.where(x == v0[k], n_slabs[k], acc))
            write_head(h, scores, v0_rows[0:1, :], v1_rows[0:1, :], z, n0, rank1)

            cnt0 = jnp.sum(jnp.where(s0 >= v0_rows[PEER_TOPK - 1:, :], 1.0, 0.0), axis=0, keepdims=True)
            cnt1 = jnp.sum(jnp.where(s1 >= v1_rows[PEER_TOPK - 1:, :], 1.0, 0.0), axis=0, keepdims=True)
            for cnt in (cnt0, cnt1, cnt2):
                suspect = jnp.maximum(suspect, jnp.abs(cnt - float(PEER_TOPK)))
            for v in (v0, v1):
                for k in range(PEER_TOPK - 1):
                    dup = jnp.where(v[k][0:1, :] == v[k + 1][0:1, :], 1.0, 0.0)
                    suspect = jnp.maximum(suspect, dup)
        return suspect

    suspect = route_heads_fast()

    @pl.when(jnp.max(suspect) > 0.0)
    def _():
        route_heads_exact()


def _peer_route(x1, mod, nw, wq_t, sk_bf, *, tile):
    b, s, d = x1.shape
    n_tok = b * s
    nblk = s // tile
    dense = lambda: pl.BlockSpec((N_PEER_HEADS, N_KEYS, tile), lambda bi, i: (0, 0, bi * nblk + i))
    full = lambda a: pl.BlockSpec(a.shape, lambda bi, i: (0,) * a.ndim)
    pair_shape = jax.ShapeDtypeStruct((N_PEER_HEADS, N_KEYS, n_tok), jnp.uint32)
    bf_shape = jax.ShapeDtypeStruct((N_PEER_HEADS, N_KEYS, n_tok), BF16)
    return pl.pallas_call(
        functools.partial(_route_kernel, tile=tile),
        grid=(b, nblk),
        in_specs=[
            pl.BlockSpec((1, tile, d), lambda bi, i: (bi, i, 0)),
            pl.BlockSpec((1, N_ADA, d), lambda bi, i: (bi, 0, 0)),
            full(nw), full(wq_t), full(sk_bf),
        ],
        out_specs=[pl.BlockSpec((1, tile, d), lambda bi, i: (bi, i, 0)),
                   dense(), dense(), dense(), dense()],
        out_shape=[jax.ShapeDtypeStruct((b, s, d), BF16),
                   pair_shape, pair_shape, bf_shape, bf_shape],
        compiler_params=_cparams("parallel", "parallel"),
        name="peer_route",
    )(x1, mod, nw, wq_t, sk_bf)


def _expert_kernel(xf_ref, e0_ref, n0_ref, e1_ref, r1_ref, u_ref, vt_ref, x_ref, mod_ref, fnw_ref,
                   o_ref, h_scr, act_scr, y_scr, *, tile, keys_per_chunk, final_norm):
    ec = pl.program_id(2)
    n_chunks = pl.num_programs(2)

    @pl.when(ec == 0)
    def _():
        y_scr[...] = jnp.zeros(y_scr.shape, F32)

    h_scr[...] = _nt_dot(u_ref[...], xf_ref[0])

    def row_bf16(ref, h, key):
        word = ref[h, pl.ds(key, 1), :]
        return pltpu.bitcast(jnp.broadcast_to(word, (N_KEYS // 2, tile)), BF16)

    for ii in range(keys_per_chunk):
        key = ec * keys_per_chunk + ii
        gate = jnp.zeros((N_KEYS, tile), BF16)
        for h in range(N_PEER_HEADS):
            e0 = row_bf16(e0_ref, h, key)
            n0 = row_bf16(n0_ref, h, key)
            gate = gate + jnp.where(r1_ref[h] < n0, e0 * e1_ref[h], jnp.zeros((), BF16))
        hid = h_scr[ii * N_KEYS:(ii + 1) * N_KEYS, :]
        act = 0.5 * hid * (1.0 + lax.erf(hid * (1.0 / math.sqrt(2.0))))
        act_scr[ii * N_KEYS:(ii + 1) * N_KEYS, :] = act.astype(BF16) * gate

    y_scr[...] += jnp.dot(vt_ref[...], act_scr[...], preferred_element_type=F32)

    @pl.when(ec == n_chunks - 1)
    def _():
        g2 = mod_ref[0, 5:6, :]
        out = x_ref[0] + g2 * y_scr[...].T
        if final_norm:
            out = out * lax.rsqrt(jnp.mean(out * out, axis=-1, keepdims=True) + EPS) * fnw_ref[...]
        o_ref[0] = out


def _peer_experts(xf, e0, n0, e1, r1, u_bf, vt_bf, x1, mod, fnw, *, tile, keys_per_chunk,
                  final_norm):
    b, s, d = x1.shape
    nblk = s // tile
    n_exp = u_bf.shape[0]
    chunk = keys_per_chunk * N_KEYS
    dense = lambda: pl.BlockSpec((N_PEER_HEADS, N_KEYS, tile),
                                 lambda bi, i, e: (0, 0, bi * nblk + i))
    tok = lambda: pl.BlockSpec((1, tile, d), lambda bi, i, e: (bi, i, 0))
    return pl.pallas_call(
        functools.partial(_expert_kernel, tile=tile, keys_per_chunk=keys_per_chunk,
                          final_norm=final_norm),
        grid=(b, nblk, n_exp // chunk),
        in_specs=[
            tok(), dense(), dense(), dense(), dense(),
            pl.BlockSpec((chunk, d), lambda bi, i, e: (e, 0)),
            pl.BlockSpec((d, chunk), lambda bi, i, e: (0, e)),
            tok(),
            pl.BlockSpec((1, N_ADA, d), lambda bi, i, e: (bi, 0, 0)),
            pl.BlockSpec(fnw.shape, lambda bi, i, e: (0, 0)),
        ],
        out_specs=tok(),
        out_shape=jax.ShapeDtypeStruct(x1.shape, F32),
        scratch_shapes=[
            pltpu.VMEM((chunk, tile), F32),
            pltpu.VMEM((chunk, tile), BF16),
            pltpu.VMEM((d, tile), F32),
        ],
        compiler_params=_cparams("parallel", "parallel", "arbitrary"),
        name="peer_experts",
    )(xf, e0, n0, e1, r1, u_bf, vt_bf, x1, mod, fnw)


def _rope_tables(seq):
    t = jnp.arange(seq)
    row = (t // GRID_W).astype(F32)
    col = (t % GRID_W).astype(F32)
    inv = 1.0 / (ROPE_THETA ** (jnp.arange(0, ROPE_HALF, 2, dtype=F32) / ROPE_HALF))
    ang_r = row[:, None] * inv
    ang_c = col[:, None] * inv
    cos_t = jnp.concatenate([jnp.cos(ang_r), jnp.cos(ang_r), jnp.cos(ang_c), jnp.cos(ang_c)], axis=1)
    sin_t = jnp.concatenate([-jnp.sin(ang_r), jnp.sin(ang_r), -jnp.sin(ang_c), jnp.sin(ang_c)], axis=1)
    return cos_t, sin_t


def _pick_tile(seq, want):
    tile = min(want, seq)
    assert seq % tile == 0
    return tile


def kernel(x, c, ada_w, ada_b, norm_mix_w, w_in, q_norm_w, k_norm_w, w_attn_o, pool_mix_w, pool_scale, w_pool_o, w_out, norm_ffn_w, w_query, sub_keys, expert_u, expert_v, final_norm_w):
    b, s, d = x.shape
    depth = ada_w.shape[0]
    cos_t, sin_t = _rope_tables(s)

    rows = 8
    c_pad = jnp.zeros((rows, d), F32).at[:b].set(c)
    mod_all = _ada_mod(c_pad, ada_w, ada_b)[:, :b].reshape(depth, b, N_ADA, d)

    t_proj = _pick_tile(s, 512)
    t_q = _pick_tile(s, 256)
    single_block_bytes = s * GROUP * t_q * 6
    t_k = s if single_block_bytes <= ATTN_SCORE_VMEM_BYTES else _pick_tile(s, 512)
    t_mix = _pick_tile(s, 512)
    t_route = _pick_tile(s, 256)
    t_exp = _pick_tile(s, 512)

    for l in range(depth):
        mod = mod_all[l]
        q, k, vt, u, sga, sgp = _in_proj(
            x, mod, norm_mix_w[l][None], w_in[l].astype(BF16), q_norm_w[l][None],
            k_norm_w[l][None], cos_t, sin_t, tile=t_proj, tk=t_k)
        attn = _flash_attention(q, k, vt, tq=t_q, tk=t_k)
        x1 = _mix_out(attn, u, sga, sgp, x, mod, w_attn_o[l].astype(BF16),
                      pool_mix_w[l].astype(BF16), pool_scale[l][None],
                      w_pool_o[l].astype(BF16), w_out[l].astype(BF16), tile=t_mix)
        xf, e0, n0, e1, r1 = _peer_route(
            x1, mod, norm_ffn_w[l][None], w_query[l].T.astype(BF16),
            sub_keys[l].astype(BF16), tile=t_route)
        x = _peer_experts(
            xf, e0, n0, e1, r1, expert_u[l].astype(BF16), expert_v[l].T.astype(BF16), x1, mod,
            final_norm_w[None], tile=t_exp, keys_per_chunk=16, final_norm=(l == depth - 1))
    return x
```
